```python
import jax, jax.numpy as jnp
from jax import lax
import numpy as np

D_MODEL = 2048
BATCH = 2
SEQ = 16384
DEPTH = 2
DEC_BATCH = 8
DEC_SEQ = 64
PAST_LEN = 1024

CHUNK = 64
GMLP_CHUNK = 128
D_MIX = D_MODEL
D_A = D_MIX // 2
D_B = D_MIX - D_A
H_A = 8
HD_A = D_A // H_A
H_B = 8
CONV_K = 3
D_FF = 5632
D_IN = 2 * D_A + 3 * D_B
EPS = 1e-6

kernel_name = "hybrid_gmlp_shortconv_streaming_step"


def _rmsnorm(x, g):
    xf = x.astype(jnp.float32)
    y = xf * lax.rsqrt(jnp.mean(xf * xf, axis=-1, keepdims=True) + EPS)
    return (y * g.astype(jnp.float32)).astype(x.dtype)


def _causal_dwconv(x, prev, w):
    t = x.shape[1]
    xp = jnp.concatenate([prev.astype(x.dtype), x], axis=1)
    w = w.astype(x.dtype)
    y = w[0] * xp[:, 0:t]
    for k in range(1, CONV_K):
        y = y + w[k] * xp[:, k:k + t]
    return y, xp[:, -(CONV_K - 1):]


def _gmlp_spatial(u, v, ws, bs):
    b, t, _ = v.shape
    n = -(-t // GMLP_CHUNK)
    pad = n * GMLP_CHUNK - t
    vp = jnp.pad(v, ((0, 0), (0, pad), (0, 0))).reshape(b, n, GMLP_CHUNK, H_A, HD_A)
    blk = jnp.arange(GMLP_CHUNK) // CHUNK
    mask = blk[:, None] >= blk[None, :]
    wm = jnp.where(mask[None], ws, 0.0).astype(v.dtype)
    z = jnp.einsum('hts,bnshd->bnthd', wm, vp)
    z = z + jnp.transpose(bs).astype(v.dtype)[None, None, :, :, None]
    z = z.reshape(b, n * GMLP_CHUNK, D_A)[:, :t]
    return u * z


def _layer(x, sconv_prev, ffn_prev, n1_g, w_in, vnorm_g, ws, bs, sconv_w,
           onorm_a_g, onorm_b_g, w_out, n2_g, ffn_up, ffn_conv_w, ffn_down):
    bsz, t, _ = x.shape
    h = _rmsnorm(x, n1_g)
    proj = h @ w_in.astype(x.dtype)
    u, v, gb, gc, hin = jnp.split(
        proj, [D_A, 2 * D_A, 2 * D_A + D_B, 2 * D_A + 2 * D_B], axis=-1)
    u = jax.nn.gelu(u)
    v = _rmsnorm(jax.nn.gelu(v).reshape(bsz, t, H_A, HD_A),
                 vnorm_g.reshape(H_A, HD_A)).reshape(bsz, t, D_A)
    a = _gmlp_spatial(u, v, ws, bs)
    cb, sconv_new = _causal_dwconv(gc * hin, sconv_prev, sconv_w)
    bout = gb * cb
    mix = jnp.concatenate([_rmsnorm(a, onorm_a_g), _rmsnorm(bout, onorm_b_g)], axis=-1)
    x = x + mix @ w_out.astype(x.dtype)
    h2 = _rmsnorm(x, n2_g)
    up, ffn_new = _causal_dwconv(h2 @ ffn_up.astype(x.dtype), ffn_prev, ffn_conv_w)
    g, val = jnp.split(up, 2, axis=-1)
    x = x + (jax.nn.silu(g) * val) @ ffn_down.astype(x.dtype)
    return x, v, sconv_new, ffn_new


def setup_inputs(seed: int = 0) -> dict:
    key = jax.random.key(seed)
    ks = jax.random.split(key, 20)
    nrm = jax.random.normal
    f32 = jnp.float32
    return {
        "x_prompt": nrm(ks[0], (BATCH, SEQ, D_MODEL), f32),
        "x_sample": nrm(ks[1], (DEC_BATCH, DEC_SEQ, D_MODEL), f32),
        "state_sconv": nrm(ks[2], (DEPTH, DEC_BATCH, CONV_K - 1, D_B), f32),
        "state_ffnconv": nrm(ks[3], (DEPTH, DEC_BATCH, CONV_K - 1, 2 * D_FF), f32),
        "n1_g": 1.0 + 0.02 * nrm(ks[4], (DEPTH, D_MODEL), f32),
        "w_in": nrm(ks[5], (DEPTH, D_MODEL, D_IN), f32) * D_MODEL ** -0.5,
        "vnorm_g": 1.0 + 0.02 * nrm(ks[6], (DEPTH, D_A), f32),
        "gmlp_ws": nrm(ks[7], (DEPTH, H_A, GMLP_CHUNK, GMLP_CHUNK), f32) * GMLP_CHUNK ** -0.5,
        "gmlp_bs": 1.0 + 0.02 * nrm(ks[8], (DEPTH, H_A, GMLP_CHUNK), f32),
        "sconv_w": nrm(ks[9], (DEPTH, CONV_K, D_B), f32) * CONV_K ** -0.5,
        "onorm_a_g": 1.0 + 0.02 * nrm(ks[10], (DEPTH, D_A), f32),
        "onorm_b_g": 1.0 + 0.02 * nrm(ks[11], (DEPTH, D_B), f32),
        "w_out": nrm(ks[12], (DEPTH, D_MIX, D_MODEL), f32) * D_MIX ** -0.5,
        "n2_g": 1.0 + 0.02 * nrm(ks[13], (DEPTH, D_MODEL), f32),
        "ffn_up": nrm(ks[14], (DEPTH, D_MODEL, 2 * D_FF), f32) * D_MODEL ** -0.5,
        "ffn_conv_w": nrm(ks[15], (DEPTH, CONV_K, 2 * D_FF), f32) * CONV_K ** -0.5,
        "ffn_down": nrm(ks[16], (DEPTH, D_FF, D_MODEL), f32) * D_FF ** -0.5,
        "final_g": 1.0 + 0.02 * nrm(ks[17], (D_MODEL,), f32),
    }


def reference(x_prompt, x_sample, state_sconv, state_ffnconv, n1_g, w_in, vnorm_g,
              gmlp_ws, gmlp_bs, sconv_w, onorm_a_g, onorm_b_g, w_out, n2_g,
              ffn_up, ffn_conv_w, ffn_down, final_g):
    xp = x_prompt
    xs = x_sample
    p_sconv, p_ffn, s_v, s_sconv, s_ffn = [], [], [], [], []
    for l in range(DEPTH):
        w = (n1_g[l], w_in[l], vnorm_g[l], gmlp_ws[l], gmlp_bs[l], sconv_w[l],
             onorm_a_g[l], onorm_b_g[l], w_out[l], n2_g[l], ffn_up[l],
             ffn_conv_w[l], ffn_down[l])
        zs = jnp.zeros((xp.shape[0], CONV_K - 1, D_B), xp.dtype)
        zf = jnp.zeros((xp.shape[0], CONV_K - 1, 2 * D_FF), xp.dtype)
        xp, _, ps, pf = _layer(xp, zs, zf, *w)
        p_sconv.append(ps)
        p_ffn.append(pf)
        xs, sv, ss, sf = _layer(xs, state_sconv[l], state_ffnconv[l], *w)
        s_v.append(sv)
        s_sconv.append(ss)
        s_ffn.append(sf)
    y_prompt = _rmsnorm(xp, final_g)
    y_sample = _rmsnorm(xs, final_g)
    return (y_prompt, y_sample, jnp.stack(p_sconv), jnp.stack(p_ffn),
            jnp.stack(s_v), jnp.stack(s_sconv), jnp.stack(s_ffn))
```

```python
import functools
import math

import jax
import jax.numpy as jnp
from jax import lax
from jax.experimental import pallas as pl
from jax.experimental.pallas import tpu as pltpu

EPS = 1e-6
H_A = 8
HD_A = 128
CHUNK = 64
GMLP_CHUNK = 128
CONV_K = 3

ROW_TILE = 512
MIX_PIECE = 256
FF_TILE = 512
VMEM_LIMIT_BYTES = 56 * 1024 * 1024

_BF16 = jnp.bfloat16
_F32 = jnp.float32


def _dot(a, b):
    return jnp.dot(a, b, preferred_element_type=_F32)


def _gelu_tanh(x):
    c = math.sqrt(2.0 / math.pi)
    return 0.5 * x * (1.0 + jnp.tanh(c * (x + 0.044715 * (x * x * x))))


def _causal_conv3(p, prev, w):
    rows = p.shape[0]
    r = lax.broadcasted_iota(jnp.int32, (rows, 1), 0)
    m1 = pltpu.roll(p, 1, 0)
    m1 = jnp.where(r == 0, prev[1:2], m1)
    m2 = pltpu.roll(p, 2, 0)
    m2 = jnp.where(r == 0, prev[0:1], jnp.where(r == 1, prev[1:2], m2))
    return w[0:1] * m2 + w[1:2] * m1 + w[2:3] * p


def _conv_segments(p, state_ref, carry, i, tiles_per_seg, w, new_state_ref, carry_store):
    n_seg = state_ref.shape[0]
    seg_rows = p.shape[0] // n_seg
    outs = []
    for s in range(n_seg):
        ps = p[s * seg_rows:(s + 1) * seg_rows]
        if tiles_per_seg == 1:
            prev = state_ref[s]
        else:
            prev = jnp.where(i % tiles_per_seg == 0, state_ref[s], carry)
        outs.append(_causal_conv3(ps, prev, w))
        last = ps[seg_rows - 2:seg_rows]
        new_state_ref[s] = last
        if tiles_per_seg != 1:
            carry_store(last)
    return outs[0] if n_seg == 1 else jnp.concatenate(outs, axis=0)


def _mixer_kernel(x_ref, state_ref, n1g_ref, wu_ref, wv_ref, wgb_ref, wgc_ref, whin_ref,
                  vng_ref, ws_ref, bias_ref, cw_ref, ona_ref, onb_ref, woa_ref, wob_ref,
                  *rest, tiles_per_seg, emit_v):
    if emit_v:
        out_ref, newstate_ref, v_ref, h_ref, acca_ref, accb_ref, ssa_ref, ssb_ref, carry_ref = rest
    else:
        out_ref, newstate_ref, h_ref, acca_ref, accb_ref, ssa_ref, ssb_ref, carry_ref = rest
        v_ref = None
    i = pl.program_id(0)
    j = pl.program_id(1)
    tm = x_ref.shape[0]

    @pl.when(j == 0)
    def _():
        x = x_ref[...]
        ms = jnp.mean(x * x, axis=-1, keepdims=True)
        h_ref[...] = (x * lax.rsqrt(ms + EPS) * n1g_ref[...]).astype(_BF16)
        acca_ref[...] = jnp.zeros_like(acca_ref)
        accb_ref[...] = jnp.zeros_like(accb_ref)
        ssa_ref[...] = jnp.zeros_like(ssa_ref)
        ssb_ref[...] = jnp.zeros_like(ssb_ref)

    h = h_ref[...]

    u = _gelu_tanh(_dot(h, wu_ref[...]))
    v = _gelu_tanh(_dot(h, wv_ref[...]))
    vng = vng_ref[...]
    heads = []
    for hh in range(MIX_PIECE // HD_A):
        vh = v[:, hh * HD_A:(hh + 1) * HD_A]
        ms = jnp.mean(vh * vh, axis=-1, keepdims=True)
        heads.append(vh * lax.rsqrt(ms + EPS) * vng[:, hh * HD_A:(hh + 1) * HD_A])
    vn = jnp.concatenate(heads, axis=-1)
    if emit_v:
        v_ref[...] = vn
    vb = vn.astype(_BF16)
    g_rows = ws_ref.shape[1]
    bias = bias_ref[...]
    z_chunks = []
    for c in range(tm // g_rows):
        zh = [_dot(ws_ref[hh], vb[c * g_rows:(c + 1) * g_rows, hh * HD_A:(hh + 1) * HD_A])
              for hh in range(MIX_PIECE // HD_A)]
        z_chunks.append(jnp.concatenate(zh, axis=-1) + bias)
    a = u * jnp.concatenate(z_chunks, axis=0)

    gb = _dot(h, wgb_ref[...])
    p = _dot(h, wgc_ref[...]) * _dot(h, whin_ref[...])

    def carry_store(last):
        carry_ref[j] = last

    cb = _conv_segments(p, state_ref, carry_ref[j], i, tiles_per_seg, cw_ref[...],
                        newstate_ref, carry_store)
    b = gb * cb

    ssa_ref[...] += jnp.sum(a * a, axis=-1, keepdims=True)
    ssb_ref[...] += jnp.sum(b * b, axis=-1, keepdims=True)
    acca_ref[...] += _dot((a * ona_ref[...]).astype(_BF16), woa_ref[...])
    accb_ref[...] += _dot((b * onb_ref[...]).astype(_BF16), wob_ref[...])

    @pl.when(j == pl.num_programs(1) - 1)
    def _():
        n_a = MIX_PIECE * pl.num_programs(1)
        ra = lax.rsqrt(ssa_ref[...] / n_a + EPS)
        rb = lax.rsqrt(ssb_ref[...] / n_a + EPS)
        out_ref[...] = x_ref[...] + ra * acca_ref[...] + rb * accb_ref[...]


def _mixer_call(x, state, tiles_per_seg, emit_v, n1_g, w_in, vnorm_g, ws_masked, bias_full,
                sconv_w, onorm_a_g, onorm_b_g, w_out):
    rows, d_model = x.shape
    n_batch, _, d_b = state.shape
    d_a = vnorm_g.shape[1]
    assert d_a == d_b and d_a % MIX_PIECE == 0
    tm = min(ROW_TILE, rows)
    assert rows % tm == 0
    ni, nj = rows // tm, d_a // MIX_PIECE
    segs_per_tile = n_batch * tiles_per_seg // ni if tiles_per_seg == 1 else 1
    assert segs_per_tile * ni == n_batch * tiles_per_seg
    g_rows = ws_masked.shape[1]
    heads_per_piece = MIX_PIECE // HD_A

    def w_in_spec(k):
        return pl.BlockSpec((d_model, MIX_PIECE), lambda i, j, k=k: (0, k * nj + j))

    def vec_spec(n):
        return pl.BlockSpec((n, MIX_PIECE), lambda i, j: (0, j))

    state_spec = pl.BlockSpec((segs_per_tile, CONV_K - 1, MIX_PIECE),
                              lambda i, j: (i // tiles_per_seg, 0, j))
    row_spec = pl.BlockSpec((tm, d_model), lambda i, j: (i, 0))
    in_specs = [
        row_spec, state_spec,
        pl.BlockSpec((1, d_model), lambda i, j: (0, 0)),
        w_in_spec(0), w_in_spec(1), w_in_spec(2), w_in_spec(3), w_in_spec(4),
        vec_spec(1),
        pl.BlockSpec((heads_per_piece, g_rows, g_rows), lambda i, j: (j, 0, 0)),
        vec_spec(g_rows), vec_spec(CONV_K), vec_spec(1), vec_spec(1),
        pl.BlockSpec((MIX_PIECE, d_model), lambda i, j: (j, 0)),
        pl.BlockSpec((MIX_PIECE, d_model), lambda i, j: (nj + j, 0)),
    ]
    out_shape = [jax.ShapeDtypeStruct((rows, d_model), _F32),
                 jax.ShapeDtypeStruct((ni * segs_per_tile, CONV_K - 1, d_b), _F32)]
    out_specs = [row_spec, pl.BlockSpec((segs_per_tile, CONV_K - 1, MIX_PIECE),
                                        lambda i, j: (i, 0, j))]
    if emit_v:
        out_shape.append(jax.ShapeDtypeStruct((rows, d_a), _F32))
        out_specs.append(pl.BlockSpec((tm, MIX_PIECE), lambda i, j: (i, j)))
    scratch = [
        pltpu.VMEM((tm, d_model), _BF16),
        pltpu.VMEM((tm, d_model), _F32), pltpu.VMEM((tm, d_model), _F32),
        pltpu.VMEM((tm, 1), _F32), pltpu.VMEM((tm, 1), _F32),
        pltpu.VMEM((nj, CONV_K - 1, MIX_PIECE), _F32),
    ]
    outs = pl.pallas_call(
        functools.partial(_mixer_kernel, tiles_per_seg=tiles_per_seg, emit_v=emit_v),
        grid=(ni, nj), in_specs=in_specs, out_specs=out_specs, out_shape=out_shape,
        scratch_shapes=scratch, name="mixer",
        compiler_params=pltpu.CompilerParams(
            dimension_semantics=("arbitrary", "arbitrary"),
            vmem_limit_bytes=VMEM_LIMIT_BYTES),
    )(x, state, n1_g, w_in, w_in, w_in, w_in, w_in, vnorm_g, ws_masked, bias_full, sconv_w,
      onorm_a_g, onorm_b_g, w_out, w_out)
    return (outs[0], _last_tile_state(outs[1], n_batch)) + tuple(outs[2:])


def _last_tile_state(per_tile_state, n_batch):
    _, k, c = per_tile_state.shape
    return per_tile_state.reshape(n_batch, -1, k, c)[:, -1]


def _ffn_kernel(x_ref, stg_ref, stv_ref, n2g_ref, wg_ref, wv_ref, cwg_ref, cwv_ref, wd_ref,
                fg_ref, out_ref, newg_ref, newv_ref, h_ref, carry_ref, *, tiles_per_seg,
                final_norm):
    i = pl.program_id(0)
    j = pl.program_id(1)

    @pl.when(j == 0)
    def _():
        x = x_ref[...]
        ms = jnp.mean(x * x, axis=-1, keepdims=True)
        h_ref[...] = (x * lax.rsqrt(ms + EPS) * n2g_ref[...]).astype(_BF16)
        out_ref[...] = x

    h = h_ref[...]

    def carry_store_g(last):
        carry_ref[0, j] = last

    def carry_store_v(last):
        carry_ref[1, j] = last

    g = _conv_segments(_dot(h, wg_ref[...]), stg_ref, carry_ref[0, j], i, tiles_per_seg,
                       cwg_ref[...], newg_ref, carry_store_g)
    val = _conv_segments(_dot(h, wv_ref[...]), stv_ref, carry_ref[1, j], i, tiles_per_seg,
                         cwv_ref[...], newv_ref, carry_store_v)
    act = (g * jax.nn.sigmoid(g) * val).astype(_BF16)
    out_ref[...] += _dot(act, wd_ref[...])

    if final_norm:
        @pl.when(j == pl.num_programs(1) - 1)
        def _():
            y = out_ref[...]
            ms = jnp.mean(y * y, axis=-1, keepdims=True)
            out_ref[...] = y * lax.rsqrt(ms + EPS) * fg_ref[...]


def _ffn_call(x, state, tiles_per_seg, final_norm, n2_g, ffn_up, ffn_conv_w, ffn_down, final_g):
    rows, d_model = x.shape
    n_batch = state.shape[0]
    d_ff = ffn_down.shape[0]
    assert d_ff % FF_TILE == 0
    tm = min(ROW_TILE, rows)
    assert rows % tm == 0
    ni, nj = rows // tm, d_ff // FF_TILE
    segs_per_tile = n_batch * tiles_per_seg // ni if tiles_per_seg == 1 else 1
    assert segs_per_tile * ni == n_batch * tiles_per_seg

    row_spec = pl.BlockSpec((tm, d_model), lambda i, j: (i, 0))
    full_vec = pl.BlockSpec((1, d_model), lambda i, j: (0, 0))

    def half_spec(shape, half, fn):
        return pl.BlockSpec(shape, lambda i, j: fn(i, half * nj + j))

    def state_spec(half):
        return half_spec((segs_per_tile, CONV_K - 1, FF_TILE), half,
                         lambda i, c: (i // tiles_per_seg, 0, c))

    in_specs = [
        row_spec, state_spec(0), state_spec(1), full_vec,
        half_spec((d_model, FF_TILE), 0, lambda i, c: (0, c)),
        half_spec((d_model, FF_TILE), 1, lambda i, c: (0, c)),
        half_spec((CONV_K, FF_TILE), 0, lambda i, c: (0, c)),
        half_spec((CONV_K, FF_TILE), 1, lambda i, c: (0, c)),
        pl.BlockSpec((FF_TILE, d_model), lambda i, j: (j, 0)),
        full_vec,
    ]
    half_state = jax.ShapeDtypeStruct((ni * segs_per_tile, CONV_K - 1, d_ff), _F32)
    new_state_spec = pl.BlockSpec((segs_per_tile, CONV_K - 1, FF_TILE), lambda i, j: (i, 0, j))
    out_shape = [jax.ShapeDtypeStruct((rows, d_model), _F32), half_state, half_state]
    out_specs = [row_spec, new_state_spec, new_state_spec]
    scratch = [pltpu.VMEM((tm, d_model), _BF16),
               pltpu.VMEM((2, nj, CONV_K - 1, FF_TILE), _F32)]
    y, new_g, new_v = pl.pallas_call(
        functools.partial(_ffn_kernel, tiles_per_seg=tiles_per_seg, final_norm=final_norm),
        grid=(ni, nj), in_specs=in_specs, out_specs=out_specs, out_shape=out_shape,
        scratch_shapes=scratch, name="ffn",
        compiler_params=pltpu.CompilerParams(
            dimension_semantics=("arbitrary", "arbitrary"),
            vmem_limit_bytes=VMEM_LIMIT_BYTES),
    )(x, state, state, n2_g, ffn_up, ffn_up, ffn_conv_w, ffn_conv_w, ffn_down, final_g)
    return y, jnp.concatenate([_last_tile_state(new_g, n_batch),
                               _last_tile_state(new_v, n_batch)], axis=-1)


def _spatial_operands(ws, bs, t):
    blk = jnp.arange(GMLP_CHUNK) // CHUNK
    mask = blk[:, None] >= blk[None, :]
    wm = jnp.where(mask[None], ws, 0.0)[:, :t, :t].astype(_BF16)
    bias = jnp.repeat(jnp.transpose(bs)[:t], HD_A, axis=1)
    return wm, bias


def kernel(x_prompt, x_sample, state_sconv, state_ffnconv, n1_g, w_in, vnorm_g, gmlp_ws, gmlp_bs,
           sconv_w, onorm_a_g, onorm_b_g, w_out, n2_g, ffn_up, ffn_conv_w, ffn_down, final_g):
    depth = w_in.shape[0]
    batch, seq, d_model = x_prompt.shape
    dec_batch, dec_seq, _ = x_sample.shape
    d_b = state_sconv.shape[-1]
    d_ff2 = state_ffnconv.shape[-1]
    assert seq % ROW_TILE == 0 and ROW_TILE % GMLP_CHUNK == 0
    assert dec_seq <= GMLP_CHUNK and (dec_batch * dec_seq) % dec_seq == 0

    xp = x_prompt.reshape(batch * seq, d_model)
    xs = x_sample.reshape(dec_batch * dec_seq, d_model)
    prompt_tiles_per_seg = seq // ROW_TILE
    zero_sconv = jnp.zeros((batch, CONV_K - 1, d_b), _F32)
    zero_ffn = jnp.zeros((batch, CONV_K - 1, d_ff2), _F32)
    fg = final_g.reshape(1, d_model)

    p_sconv, p_ffn, s_v, s_sconv, s_ffn = [], [], [], [], []
    for l in range(depth):
        last = l == depth - 1
        wm_p, bias_p = _spatial_operands(gmlp_ws[l], gmlp_bs[l], GMLP_CHUNK)
        wm_s, bias_s = _spatial_operands(gmlp_ws[l], gmlp_bs[l], dec_seq)
        mixer_w = (n1_g[l].reshape(1, -1), w_in[l].astype(_BF16), vnorm_g[l].reshape(1, -1))
        mixer_w2 = (sconv_w[l], onorm_a_g[l].reshape(1, -1), onorm_b_g[l].reshape(1, -1),
                    w_out[l].astype(_BF16))
        ffn_w = (n2_g[l].reshape(1, -1), ffn_up[l].astype(_BF16), ffn_conv_w[l],
                 ffn_down[l].astype(_BF16), fg)

        xp, ps = _mixer_call(xp, zero_sconv, prompt_tiles_per_seg, False, *mixer_w, wm_p, bias_p,
                             *mixer_w2)
        xp, pf = _ffn_call(xp, zero_ffn, prompt_tiles_per_seg, last, *ffn_w)
        xs, ss, sv = _mixer_call(xs, state_sconv[l], 1, True, *mixer_w, wm_s, bias_s, *mixer_w2)
        xs, sf = _ffn_call(xs, state_ffnconv[l], 1, last, *ffn_w)
        p_sconv.append(ps)
        p_ffn.append(pf)
        s_v.append(sv.reshape(dec_batch, dec_seq, -1))
        s_sconv.append(ss)
        s_ffn.append(sf)

    return (xp.reshape(batch, seq, d_model), xs.reshape(dec_batch, dec_seq, d_model),
            jnp.stack(p_sconv), jnp.stack(p_ffn), jnp.stack(s_v), jnp.stack(s_sconv),
            jnp.stack(s_ffn))
```

```python
import functools
import math

import jax
import jax.numpy as jnp
from jax import lax
from jax.experimental import pallas as pl
from jax.experimental.pallas import tpu as pltpu

EPS = 1e-6
H_A = 8
HD_A = 128
CHUNK = 64
GMLP_CHUNK = 128
CONV_K = 3
SUBLANES = 8
LANES = 128
CONV_PHASES = 4
HEAD_ROWS = SUBLANES

ROW_TILE = 512
MIX_PIECE = 256
FF_TILE = 512
VMEM_LIMIT_BYTES = 56 * 1024 * 1024

_BF16 = jnp.bfloat16
_F32 = jnp.float32


def _dot(a, b):
    return jnp.dot(a, b, preferred_element_type=_F32)


def _gelu_tanh(x):
    c = math.sqrt(2.0 / math.pi)
    return 0.5 * x * (1.0 + jnp.tanh(c * (x + 0.044715 * (x * x * x))))


def _causal_conv3(p, prev, w):
    r = lax.broadcasted_iota(jnp.int32, (SUBLANES, 1), 0)
    m1 = pltpu.roll(p, 1, 0)
    m2 = pltpu.roll(p, 2, 0)
    h1 = jnp.where(r == 0, prev[1:2], m1[:SUBLANES])
    h2 = jnp.where(r == 0, prev[0:1], jnp.where(r == 1, prev[1:2], m2[:SUBLANES]))
    m1 = jnp.concatenate([h1, m1[SUBLANES:]], axis=0)
    m2 = jnp.concatenate([h2, m2[SUBLANES:]], axis=0)
    return w[0:1] * m2 + w[1:2] * m1 + w[2:3] * p


def _slab_seg_rows(slab_ref, n_seg):
    return slab_ref.shape[1] // n_seg - HEAD_ROWS


def _slab_store(slab_ref, val, n_seg):
    seg_rows = _slab_seg_rows(slab_ref, n_seg)
    for c in range(slab_ref.shape[0]):
        for s in range(n_seg):
            r0 = s * (HEAD_ROWS + seg_rows) + HEAD_ROWS
            slab_ref[c, r0:r0 + seg_rows, :] = val[s * seg_rows:(s + 1) * seg_rows,
                                                   c * LANES:(c + 1) * LANES]


def _slab_conv_prepare(slab_ref, state_ref, carry_ref, first_tile, tiles_per_seg, new_state_ref):
    n_seg = state_ref.shape[0]
    seg_rows = _slab_seg_rows(slab_ref, n_seg)
    for s in range(n_seg):
        if tiles_per_seg == 1:
            prev = state_ref[s]
        else:
            prev = jnp.where(first_tile, state_ref[s], carry_ref[...])
        r0 = s * (HEAD_ROWS + seg_rows) + HEAD_ROWS
        lasts = []
        for c in range(slab_ref.shape[0]):
            slab_ref[c, r0 - (CONV_K - 1):r0, :] = prev[:, c * LANES:(c + 1) * LANES]
            lasts.append(slab_ref[c, r0 + seg_rows - (CONV_K - 1):r0 + seg_rows, :])
        last = jnp.concatenate(lasts, axis=-1)
        new_state_ref[s] = last
        if tiles_per_seg != 1:
            carry_ref[...] = last


def _slab_conv_phase(slab_ref, c, seg, n_seg, q, w):
    seg_rows = _slab_seg_rows(slab_ref, n_seg)
    base = seg * (HEAD_ROWS + seg_rows) + HEAD_ROWS + q
    n = seg_rows // CONV_PHASES
    t0, t1, t2 = [slab_ref[c, pl.ds(base - d, n, stride=CONV_PHASES), :] for d in range(CONV_K)]
    return w[2:3] * t0 + w[1:2] * t1 + w[0:1] * t2


def _phase_store(dst_ref, c, seg, n_seg, q, val):
    seg_rows = dst_ref.shape[1] // n_seg
    dst_ref[c, pl.ds(seg * seg_rows + q, seg_rows // CONV_PHASES, stride=CONV_PHASES), :] = val


def _conv_segments(p, state_ref, carry, i, tiles_per_seg, w, new_state_ref, carry_store):
    n_seg = state_ref.shape[0]
    seg_rows = p.shape[0] // n_seg
    outs = []
    for s in range(n_seg):
        ps = p[s * seg_rows:(s + 1) * seg_rows]
        if tiles_per_seg == 1:
            prev = state_ref[s]
        else:
            prev = jnp.where(i % tiles_per_seg == 0, state_ref[s], carry)
        outs.append(_causal_conv3(ps, prev, w))
        last = ps[seg_rows - 2:seg_rows]
        new_state_ref[s] = last
        if tiles_per_seg != 1:
            carry_store(last)
    return outs[0] if n_seg == 1 else jnp.concatenate(outs, axis=0)


def _mixer_kernel(x_ref, state_ref, n1g_ref, wu_ref, wv_ref, wgb_ref, wgc_ref, whin_ref,
                  vng_ref, ws_ref, bias_ref, cw_ref, ona_ref, onb_ref, woa_ref, wob_ref,
                  *rest, tiles_per_seg, emit_v):
    if emit_v:
        out_ref, newstate_ref, v_ref, h_ref, acca_ref, accb_ref, ssa_ref, ssb_ref, carry_ref = rest
    else:
        out_ref, newstate_ref, h_ref, acca_ref, accb_ref, ssa_ref, ssb_ref, carry_ref = rest
        v_ref = None
    i = pl.program_id(0)
    j = pl.program_id(1)
    tm = x_ref.shape[0]

    @pl.when(j == 0)
    def _():
        x = x_ref[...]
        ms = jnp.mean(x * x, axis=-1, keepdims=True)
        h_ref[...] = (x * lax.rsqrt(ms + EPS) * n1g_ref[...]).astype(_BF16)
        acca_ref[...] = jnp.zeros_like(acca_ref)
        accb_ref[...] = jnp.zeros_like(accb_ref)
        ssa_ref[...] = jnp.zeros_like(ssa_ref)
        ssb_ref[...] = jnp.zeros_like(ssb_ref)

    h = h_ref[...]

    u = _gelu_tanh(_dot(h, wu_ref[...]))
    v = _gelu_tanh(_dot(h, wv_ref[...]))
    vng = vng_ref[...]
    heads = []
    for hh in range(MIX_PIECE // HD_A):
        vh = v[:, hh * HD_A:(hh + 1) * HD_A]
        ms = jnp.mean(vh * vh, axis=-1, keepdims=True)
        heads.append(vh * lax.rsqrt(ms + EPS) * vng[:, hh * HD_A:(hh + 1) * HD_A])
    vn = jnp.concatenate(heads, axis=-1)
    if emit_v:
        v_ref[...] = vn
    vb = vn.astype(_BF16)
    g_rows = ws_ref.shape[1]
    bias = bias_ref[...]
    z_chunks = []
    for c in range(tm // g_rows):
        zh = [_dot(ws_ref[hh], vb[c * g_rows:(c + 1) * g_rows, hh * HD_A:(hh + 1) * HD_A])
              for hh in range(MIX_PIECE // HD_A)]
        z_chunks.append(jnp.concatenate(zh, axis=-1) + bias)
    a = u * jnp.concatenate(z_chunks, axis=0)

    gb = _dot(h, wgb_ref[...])
    p = _dot(h, wgc_ref[...]) * _dot(h, whin_ref[...])

    def carry_store(last):
        carry_ref[j] = last

    cb = _conv_segments(p, state_ref, carry_ref[j], i, tiles_per_seg, cw_ref[...],
                        newstate_ref, carry_store)
    b = gb * cb

    ssa_ref[...] += jnp.sum(a * a, axis=-1, keepdims=True)
    ssb_ref[...] += jnp.sum(b * b, axis=-1, keepdims=True)
    acca_ref[...] += _dot((a * ona_ref[...]).astype(_BF16), woa_ref[...])
    accb_ref[...] += _dot((b * onb_ref[...]).astype(_BF16), wob_ref[...])

    @pl.when(j == pl.num_programs(1) - 1)
    def _():
        n_a = MIX_PIECE * pl.num_programs(1)
        ra = lax.rsqrt(ssa_ref[...] / n_a + EPS)
        rb = lax.rsqrt(ssb_ref[...] / n_a + EPS)
        out_ref[...] = x_ref[...] + ra * acca_ref[...] + rb * accb_ref[...]


def _mixer_call(x, state, tiles_per_seg, emit_v, n1_g, w_in, vnorm_g, ws_masked, bias_full,
                sconv_w, onorm_a_g, onorm_b_g, w_out):
    rows, d_model = x.shape
    n_batch, _, d_b = state.shape
    d_a = vnorm_g.shape[1]
    assert d_a == d_b and d_a % MIX_PIECE == 0
    tm = min(ROW_TILE, rows)
    assert rows % tm == 0
    ni, nj = rows // tm, d_a // MIX_PIECE
    segs_per_tile = n_batch * tiles_per_seg // ni if tiles_per_seg == 1 else 1
    assert segs_per_tile * ni == n_batch * tiles_per_seg
    g_rows = ws_masked.shape[1]
    heads_per_piece = MIX_PIECE // HD_A

    def w_in_spec(k):
        return pl.BlockSpec((d_model, MIX_PIECE), lambda i, j, k=k: (0, k * nj + j))

    def vec_spec(n):
        return pl.BlockSpec((n, MIX_PIECE), lambda i, j: (0, j))

    state_spec = pl.BlockSpec((segs_per_tile, CONV_K - 1, MIX_PIECE),
                              lambda i, j: (i // tiles_per_seg, 0, j))
    row_spec = pl.BlockSpec((tm, d_model), lambda i, j: (i, 0))
    in_specs = [
        row_spec, state_spec,
        pl.BlockSpec((1, d_model), lambda i, j: (0, 0)),
        w_in_spec(0), w_in_spec(1), w_in_spec(2), w_in_spec(3), w_in_spec(4),
        vec_spec(1),
        pl.BlockSpec((heads_per_piece, g_rows, g_rows), lambda i, j: (j, 0, 0)),
        vec_spec(g_rows), vec_spec(CONV_K), vec_spec(1), vec_spec(1),
        pl.BlockSpec((MIX_PIECE, d_model), lambda i, j: (j, 0)),
        pl.BlockSpec((MIX_PIECE, d_model), lambda i, j: (nj + j, 0)),
    ]
    out_shape = [jax.ShapeDtypeStruct((rows, d_model), _F32),
                 jax.ShapeDtypeStruct((ni * segs_per_tile, CONV_K - 1, d_b), _F32)]
    out_specs = [row_spec, pl.BlockSpec((segs_per_tile, CONV_K - 1, MIX_PIECE),
                                        lambda i, j: (i, 0, j))]
    if emit_v:
        out_shape.append(jax.ShapeDtypeStruct((rows, d_a), _F32))
        out_specs.append(pl.BlockSpec((tm, MIX_PIECE), lambda i, j: (i, j)))
    scratch = [
        pltpu.VMEM((tm, d_model), _BF16),
        pltpu.VMEM((tm, d_model), _F32), pltpu.VMEM((tm, d_model), _F32),
        pltpu.VMEM((tm, 1), _F32), pltpu.VMEM((tm, 1), _F32),
        pltpu.VMEM((nj, CONV_K - 1, MIX_PIECE), _F32),
    ]
    outs = pl.pallas_call(
        functools.partial(_mixer_kernel, tiles_per_seg=tiles_per_seg, emit_v=emit_v),
        grid=(ni, nj), in_specs=in_specs, out_specs=out_specs, out_shape=out_shape,
        scratch_shapes=scratch, name="mixer",
        compiler_params=pltpu.CompilerParams(
            dimension_semantics=("arbitrary", "arbitrary"),
            vmem_limit_bytes=VMEM_LIMIT_BYTES),
    )(x, state, n1_g, w_in, w_in, w_in, w_in, w_in, vnorm_g, ws_masked, bias_full, sconv_w,
      onorm_a_g, onorm_b_g, w_out, w_out)
    return (outs[0], _last_tile_state(outs[1], n_batch)) + tuple(outs[2:])


def _last_tile_state(per_tile_state, n_batch):
    _, k, c = per_tile_state.shape
    return per_tile_state.reshape(n_batch, -1, k, c)[:, -1]


def _ffn_kernel(x_ref, stg_ref, stv_ref, n2g_ref, wg_ref, wv_ref, cwg_ref, cwv_ref, wd_ref,
                fg_ref, out_ref, newg_ref, newv_ref, h_ref, upg_ref, upv_ref, act_ref,
                carry_ref, *,
                n_ff_tiles, tiles_per_seg, final_norm):
    s = pl.program_id(0)
    n_items = pl.num_programs(0) - 1
    ja = s % n_ff_tiles
    item_b = jnp.maximum(s - 1, 0)
    ib = item_b // n_ff_tiles
    jb = item_b % n_ff_tiles

    @pl.when(s == 0)
    def _():
        upg_ref[...] = jnp.zeros_like(upg_ref)
        upv_ref[...] = jnp.zeros_like(upv_ref)

    @pl.when(jnp.logical_and(ja == 0, s < n_items))
    def _():
        x = x_ref[...]
        ms = jnp.mean(x * x, axis=-1, keepdims=True)
        h_ref[...] = (x * lax.rsqrt(ms + EPS) * n2g_ref[...]).astype(_BF16)

    @pl.when(jb == 0)
    def _():
        out_ref[...] = x_ref[...]

    n_seg = stg_ref.shape[0]
    first_tile = ib % tiles_per_seg == 0
    _slab_conv_prepare(upg_ref, stg_ref, carry_ref.at[0, jb], first_tile, tiles_per_seg, newg_ref)
    _slab_conv_prepare(upv_ref, stv_ref, carry_ref.at[1, jb], first_tile, tiles_per_seg, newv_ref)
    cwg = cwg_ref[...]
    cwv = cwv_ref[...]
    for c in range(act_ref.shape[0]):
        lanes = slice(c * LANES, (c + 1) * LANES)
        for seg in range(n_seg):
            for q in range(CONV_PHASES):
                g = _slab_conv_phase(upg_ref, c, seg, n_seg, q, cwg[:, lanes])
                val = _slab_conv_phase(upv_ref, c, seg, n_seg, q, cwv[:, lanes])
                _phase_store(act_ref, c, seg, n_seg, q, g * jax.nn.sigmoid(g) * val)
    act = jnp.concatenate([act_ref[c] for c in range(act_ref.shape[0])], axis=-1)
    out_ref[...] += _dot(act.astype(_BF16), wd_ref[...])

    h = h_ref[...]
    _slab_store(upg_ref, _dot(h, wg_ref[...]), n_seg)
    _slab_store(upv_ref, _dot(h, wv_ref[...]), n_seg)

    if final_norm:
        @pl.when(jnp.logical_and(jb == n_ff_tiles - 1, s > 0))
        def _():
            y = out_ref[...]
            ms = jnp.mean(y * y, axis=-1, keepdims=True)
            out_ref[...] = y * lax.rsqrt(ms + EPS) * fg_ref[...]


def _ffn_call(x, state, tiles_per_seg, final_norm, n2_g, ffn_up, ffn_conv_w, ffn_down, final_g):
    rows, d_model = x.shape
    n_batch = state.shape[0]
    d_ff = ffn_down.shape[0]
    assert d_ff % FF_TILE == 0
    tm = min(ROW_TILE, rows)
    assert rows % tm == 0
    ni, nj = rows // tm, d_ff // FF_TILE
    segs_per_tile = n_batch * tiles_per_seg // ni if tiles_per_seg == 1 else 1
    assert segs_per_tile * ni == n_batch * tiles_per_seg

    assert nj >= 2
    n_items = ni * nj

    def item_a(s):
        t = jnp.minimum(s, n_items - 1)
        return t // nj, t % nj

    def item_b(s):
        t = jnp.maximum(s - 1, 0)
        return t // nj, t % nj

    def spec_a(shape, fn):
        return pl.BlockSpec(shape, lambda s: fn(*item_a(s)))

    def spec_b(shape, fn):
        return pl.BlockSpec(shape, lambda s: fn(*item_b(s)))

    full_vec = pl.BlockSpec((1, d_model), lambda s: (0, 0))
    state_shape = (segs_per_tile, CONV_K - 1, FF_TILE)
    in_specs = [
        spec_a((tm, d_model), lambda i, j: (i, 0)),
        spec_b(state_shape, lambda i, j: (i // tiles_per_seg, 0, j)),
        spec_b(state_shape, lambda i, j: (i // tiles_per_seg, 0, nj + j)),
        full_vec,
        spec_a((d_model, FF_TILE), lambda i, j: (0, j)),
        spec_a((d_model, FF_TILE), lambda i, j: (0, nj + j)),
        spec_b((CONV_K, FF_TILE), lambda i, j: (0, j)),
        spec_b((CONV_K, FF_TILE), lambda i, j: (0, nj + j)),
        spec_b((FF_TILE, d_model), lambda i, j: (j, 0)),
        full_vec,
    ]
    assert (tm // segs_per_tile) % (CONV_PHASES * SUBLANES) == 0
    slab_shape = (FF_TILE // LANES, tm + segs_per_tile * HEAD_ROWS, LANES)
    half_state = jax.ShapeDtypeStruct((ni * segs_per_tile, CONV_K - 1, d_ff), _F32)
    new_state_spec = spec_b(state_shape, lambda i, j: (i, 0, j))
    out_shape = [jax.ShapeDtypeStruct((rows, d_model), _F32), half_state, half_state]
    out_specs = [spec_b((tm, d_model), lambda i, j: (i, 0)), new_state_spec, new_state_spec]
    scratch = [pltpu.VMEM((tm, d_model), _BF16),
               pltpu.VMEM(slab_shape, _F32), pltpu.VMEM(slab_shape, _F32),
               pltpu.VMEM((FF_TILE // LANES, tm, LANES), _F32),
               pltpu.VMEM((2, nj, CONV_K - 1, FF_TILE), _F32)]
    y, new_g, new_v = pl.pallas_call(
        functools.partial(_ffn_kernel, n_ff_tiles=nj, tiles_per_seg=tiles_per_seg,
                          final_norm=final_norm),
        grid=(n_items + 1,), in_specs=in_specs, out_specs=out_specs, out_shape=out_shape,
        scratch_shapes=scratch, name="ffn",
        compiler_params=pltpu.CompilerParams(
            dimension_semantics=("arbitrary",),
            vmem_limit_bytes=VMEM_LIMIT_BYTES),
    )(x, state, state, n2_g, ffn_up, ffn_up, ffn_conv_w, ffn_conv_w, ffn_down, final_g)
    return y, jnp.concatenate([_last_tile_state(new_g, n_batch),
                               _last_tile_state(new_v, n_batch)], axis=-1)


def _spatial_operands(ws, bs, t):
    blk = jnp.arange(GMLP_CHUNK) // CHUNK
    mask = blk[:, None] >= blk[None, :]
    wm = jnp.where(mask[None], ws, 0.0)[:, :t, :t].astype(_BF16)
    bias = jnp.repeat(jnp.transpose(bs)[:t], HD_A, axis=1)
    return wm, bias


def kernel(x_prompt, x_sample, state_sconv, state_ffnconv, n1_g, w_in, vnorm_g, gmlp_ws, gmlp_bs,
           sconv_w, onorm_a_g, onorm_b_g, w_out, n2_g, ffn_up, ffn_conv_w, ffn_down, final_g):
    depth = w_in.shape[0]
    batch, seq, d_model = x_prompt.shape
    dec_batch, dec_seq, _ = x_sample.shape
    d_b = state_sconv.shape[-1]
    d_ff2 = state_ffnconv.shape[-1]
    assert seq % ROW_TILE == 0 and ROW_TILE % GMLP_CHUNK == 0
    assert dec_seq <= GMLP_CHUNK and (dec_batch * dec_seq) % dec_seq == 0

    xp = x_prompt.reshape(batch * seq, d_model)
    xs = x_sample.reshape(dec_batch * dec_seq, d_model)
    prompt_tiles_per_seg = seq // ROW_TILE
    zero_sconv = jnp.zeros((batch, CONV_K - 1, d_b), _F32)
    zero_ffn = jnp.zeros((batch, CONV_K - 1, d_ff2), _F32)
    fg = final_g.reshape(1, d_model)

    p_sconv, p_ffn, s_v, s_sconv, s_ffn = [], [], [], [], []
    for l in range(depth):
        last = l == depth - 1
        wm_p, bias_p = _spatial_operands(gmlp_ws[l], gmlp_bs[l], GMLP_CHUNK)
        wm_s, bias_s = _spatial_operands(gmlp_ws[l], gmlp_bs[l], dec_seq)
        mixer_w = (n1_g[l].reshape(1, -1), w_in[l].astype(_BF16), vnorm_g[l].reshape(1, -1))
        mixer_w2 = (sconv_w[l], onorm_a_g[l].reshape(1, -1), onorm_b_g[l].reshape(1, -1),
                    w_out[l].astype(_BF16))
        ffn_w = (n2_g[l].reshape(1, -1), ffn_up[l].astype(_BF16), ffn_conv_w[l],
                 ffn_down[l].astype(_BF16), fg)

        xp, ps = _mixer_call(xp, zero_sconv, prompt_tiles_per_seg, False, *mixer_w, wm_p, bias_p,
                             *mixer_w2)
        xp, pf = _ffn_call(xp, zero_ffn, prompt_tiles_per_seg, last, *ffn_w)
        xs, ss, sv = _mixer_call(xs, state_sconv[l], 1, True, *mixer_w, wm_s, bias_s, *mixer_w2)
        xs, sf = _ffn_call(xs, state_ffnconv[l], 1, last, *ffn_w)
        p_sconv.append(ps)
        p_ffn.append(pf)
        s_v.append(sv.reshape(dec_batch, dec_seq, -1))
        s_sconv.append(ss)
        s_ffn.append(sf)

    return (xp.reshape(batch, seq, d_model), xs.reshape(dec_batch, dec_seq, d_model),
            jnp.stack(p_sconv), jnp.stack(p_ffn), jnp.stack(s_v), jnp.stack(s_sconv),
            jnp.stack(s_ffn))
```

```python
import functools
import math

import jax
import jax.numpy as jnp
from jax import lax
from jax.experimental import pallas as pl
from jax.experimental.pallas import tpu as pltpu

EPS = 1e-6
H_A = 8
HD_A = 128
CHUNK = 64
GMLP_CHUNK = 128
CONV_K = 3
SUBLANES = 8
LANES = 128
CONV_PHASES = 4
HEAD_ROWS = SUBLANES

MIX_ROW_TILE = 256
MIX_PIECE = 256
FFN_ROW_TILE = 1024
FF_TILE = 512
VMEM_LIMIT_BYTES = 56 * 1024 * 1024

_BF16 = jnp.bfloat16
_F32 = jnp.float32


def _dot(a, b):
    return jnp.dot(a, b, preferred_element_type=_F32)


def _gelu_tanh(x):
    c = math.sqrt(2.0 / math.pi)
    return 0.5 * x * (1.0 + jnp.tanh(c * (x + 0.044715 * (x * x * x))))


def _rms_scale(x):
    return lax.rsqrt(jnp.mean(x * x, axis=-1, keepdims=True) + EPS)


def _slab_seg_rows(slab_ref, n_seg):
    return slab_ref.shape[1] // n_seg - HEAD_ROWS


def _slab_store(slab_ref, val, n_seg):
    seg_rows = _slab_seg_rows(slab_ref, n_seg)
    for c in range(slab_ref.shape[0]):
        for s in range(n_seg):
            r0 = s * (HEAD_ROWS + seg_rows) + HEAD_ROWS
            slab_ref[c, r0:r0 + seg_rows, :] = val[s * seg_rows:(s + 1) * seg_rows,
                                                   c * LANES:(c + 1) * LANES]


def _slab_conv_prepare(slab_ref, state_ref, carry_ref, first_tile, tiles_per_seg, new_state_ref,
                       lane0=0):
    n_seg = state_ref.shape[0]
    seg_rows = _slab_seg_rows(slab_ref, n_seg)
    width = slab_ref.shape[0] * LANES
    for s in range(n_seg):
        if tiles_per_seg == 1:
            prev = state_ref[s, :, lane0:lane0 + width]
        else:
            prev = jnp.where(first_tile, state_ref[s, :, lane0:lane0 + width],
                             carry_ref[:, lane0:lane0 + width])
        r0 = s * (HEAD_ROWS + seg_rows) + HEAD_ROWS
        lasts = []
        for c in range(slab_ref.shape[0]):
            slab_ref[c, r0 - (CONV_K - 1):r0, :] = prev[:, c * LANES:(c + 1) * LANES]
            lasts.append(slab_ref[c, r0 + seg_rows - (CONV_K - 1):r0 + seg_rows, :])
        last = jnp.concatenate(lasts, axis=-1)
        new_state_ref[s, :, lane0:lane0 + width] = last
        if tiles_per_seg != 1:
            carry_ref[:, lane0:lane0 + width] = last


def _slab_conv_phase(slab_ref, c, seg, n_seg, q, w):
    seg_rows = _slab_seg_rows(slab_ref, n_seg)
    base = seg * (HEAD_ROWS + seg_rows) + HEAD_ROWS + q
    n = seg_rows // CONV_PHASES
    t0, t1, t2 = [slab_ref[c, pl.ds(base - d, n, stride=CONV_PHASES), :] for d in range(CONV_K)]
    return w[2:3] * t0 + w[1:2] * t1 + w[0:1] * t2


def _phase_store(dst_ref, c, seg, n_seg, q, val):
    seg_rows = dst_ref.shape[1] // n_seg
    dst_ref[c, pl.ds(seg * seg_rows + q, seg_rows // CONV_PHASES, stride=CONV_PHASES), :] = val


def _segments_per_tile(n_batch, tiles_per_seg, n_tiles):
    segs = n_batch // n_tiles if tiles_per_seg == 1 else 1
    assert segs * n_tiles == n_batch * tiles_per_seg
    return segs


def _last_tile_state(per_tile_state, n_batch):
    _, k, c = per_tile_state.shape
    return per_tile_state.reshape(n_batch, -1, k, c)[:, -1]


def _mixer_kernel(x_ref, state_ref, n1g_ref, win_ref, vng_ref, ws_ref, bias_ref, cw_ref,
                  ona_ref, onb_ref, wout_ref, *rest, tiles_per_seg, emit_v):
    if emit_v:
        out_ref, newstate_ref, v_ref, pslab_ref, cb_ref, carry_ref = rest
    else:
        out_ref, newstate_ref, pslab_ref, cb_ref, carry_ref = rest
        v_ref = None
    tm = x_ref.shape[0]
    d_a = vng_ref.shape[1]
    n_seg = state_ref.shape[0]
    slabs_per_piece = MIX_PIECE // LANES
    first_tile = pl.program_id(0) % tiles_per_seg == 0

    x = x_ref[...]
    h = (x * _rms_scale(x) * n1g_ref[...]).astype(_BF16)

    def proj(k, j):
        lo = k * d_a + j * MIX_PIECE
        return _dot(h, win_ref[:, lo:lo + MIX_PIECE])

    g_rows = ws_ref.shape[1]
    a_pieces, b_pieces = [], []
    for j in range(d_a // MIX_PIECE):
        cols = slice(j * MIX_PIECE, (j + 1) * MIX_PIECE)
        u = _gelu_tanh(proj(0, j))
        v = _gelu_tanh(proj(1, j))
        heads = [v[:, hh * HD_A:(hh + 1) * HD_A] for hh in range(MIX_PIECE // HD_A)]
        vn = jnp.concatenate([vh * _rms_scale(vh) for vh in heads], axis=-1) * vng_ref[:, cols]
        if emit_v:
            v_ref[:, cols] = vn
        vb = vn.astype(_BF16)
        bias = bias_ref[:, cols]
        z_chunks = []
        for c in range(tm // g_rows):
            zh = [_dot(ws_ref[j * (MIX_PIECE // HD_A) + hh],
                       vb[c * g_rows:(c + 1) * g_rows, hh * HD_A:(hh + 1) * HD_A])
                  for hh in range(MIX_PIECE // HD_A)]
            z_chunks.append(jnp.concatenate(zh, axis=-1) + bias)
        a_pieces.append(u * jnp.concatenate(z_chunks, axis=0))

        gb = proj(2, j)
        slab = pslab_ref.at[j * slabs_per_piece:(j + 1) * slabs_per_piece]
        _slab_store(slab, proj(3, j) * proj(4, j), n_seg)
        _slab_conv_prepare(slab, state_ref, carry_ref, first_tile, tiles_per_seg, newstate_ref,
                           lane0=j * MIX_PIECE)
        cw = cw_ref[:, cols]
        for c in range(slabs_per_piece):
            for seg in range(n_seg):
                for q in range(CONV_PHASES):
                    _phase_store(cb_ref, j * slabs_per_piece + c, seg, n_seg, q,
                                 _slab_conv_phase(slab, c, seg, n_seg, q,
                                                  cw[:, c * LANES:(c + 1) * LANES]))
        cb = jnp.concatenate([cb_ref[j * slabs_per_piece + c] for c in range(slabs_per_piece)],
                             axis=-1)
        b_pieces.append(gb * cb)

    a = jnp.concatenate(a_pieces, axis=-1)
    b = jnp.concatenate(b_pieces, axis=-1)
    mix = jnp.concatenate([a * _rms_scale(a) * ona_ref[...], b * _rms_scale(b) * onb_ref[...]],
                          axis=-1)
    out_ref[...] = x + _dot(mix.astype(_BF16), wout_ref[...])


def _mixer_call(x, state, tiles_per_seg, emit_v, n1_g, w_in, vnorm_g, ws_masked, bias_full,
                sconv_w, onorm_a_g, onorm_b_g, w_out):
    rows, d_model = x.shape
    n_batch, _, d_b = state.shape
    d_a = vnorm_g.shape[1]
    assert d_a == d_b and d_a % MIX_PIECE == 0 and w_in.shape[1] == 2 * d_a + 3 * d_b
    tm = min(MIX_ROW_TILE, rows)
    assert rows % tm == 0
    ni = rows // tm
    segs_per_tile = _segments_per_tile(n_batch, tiles_per_seg, ni)
    assert (tm // segs_per_tile) % (CONV_PHASES * SUBLANES) == 0
    g_rows = ws_masked.shape[1]
    assert (tm // segs_per_tile) % g_rows == 0

    def resident(shape):
        return pl.BlockSpec(shape, lambda i: (0,) * len(shape), pipeline_mode=pl.Buffered(1))

    row_spec = pl.BlockSpec((tm, d_model), lambda i: (i, 0))
    state_shape = (segs_per_tile, CONV_K - 1, d_b)
    in_specs = [
        row_spec,
        pl.BlockSpec(state_shape, lambda i: (i // tiles_per_seg, 0, 0)),
        resident((1, d_model)), resident(w_in.shape), resident((1, d_a)),
        resident(ws_masked.shape), resident(bias_full.shape), resident(sconv_w.shape),
        resident((1, d_a)), resident((1, d_b)), resident(w_out.shape),
    ]
    out_shape = [jax.ShapeDtypeStruct((rows, d_model), _F32),
                 jax.ShapeDtypeStruct((ni * segs_per_tile, CONV_K - 1, d_b), _F32)]
    out_specs = [row_spec, pl.BlockSpec(state_shape, lambda i: (i, 0, 0))]
    if emit_v:
        out_shape.append(jax.ShapeDtypeStruct((rows, d_a), _F32))
        out_specs.append(pl.BlockSpec((tm, d_a), lambda i: (i, 0)))
    scratch = [
        pltpu.VMEM((d_b // LANES, tm + segs_per_tile * HEAD_ROWS, LANES), _F32),
        pltpu.VMEM((d_b // LANES, tm, LANES), _F32),
        pltpu.VMEM((CONV_K - 1, d_b), _F32),
    ]
    outs = pl.pallas_call(
        functools.partial(_mixer_kernel, tiles_per_seg=tiles_per_seg, emit_v=emit_v),
        grid=(ni,), in_specs=in_specs, out_specs=out_specs, out_shape=out_shape,
        scratch_shapes=scratch, name="mixer",
        compiler_params=pltpu.CompilerParams(
            dimension_semantics=("arbitrary",),
            vmem_limit_bytes=VMEM_LIMIT_BYTES),
    )(x, state, n1_g, w_in, vnorm_g, ws_masked, bias_full, sconv_w, onorm_a_g, onorm_b_g, w_out)
    return (outs[0], _last_tile_state(outs[1], n_batch)) + tuple(outs[2:])


def _ffn_kernel(x_ref, stg_ref, stv_ref, n2g_ref, wg_ref, wv_ref, cwg_ref, cwv_ref, wd_ref,
                fg_ref, out_ref, newg_ref, newv_ref, h_ref, upg_ref, upv_ref, act_ref,
                carry_ref, *, n_ff_tiles, tiles_per_seg, final_norm):
    s = pl.program_id(0)
    n_items = pl.num_programs(0) - 1
    ja = s % n_ff_tiles
    item_b = jnp.maximum(s - 1, 0)
    ib = item_b // n_ff_tiles
    jb = item_b % n_ff_tiles

    @pl.when(s == 0)
    def _():
        upg_ref[...] = jnp.zeros_like(upg_ref)
        upv_ref[...] = jnp.zeros_like(upv_ref)

    @pl.when(jnp.logical_and(ja == 0, s < n_items))
    def _():
        x = x_ref[...]
        h_ref[...] = (x * _rms_scale(x) * n2g_ref[...]).astype(_BF16)

    @pl.when(jb == 0)
    def _():
        out_ref[...] = x_ref[...]

    n_seg = stg_ref.shape[0]
    first_tile = ib % tiles_per_seg == 0
    _slab_conv_prepare(upg_ref, stg_ref, carry_ref.at[0, jb], first_tile, tiles_per_seg, newg_ref)
    _slab_conv_prepare(upv_ref, stv_ref, carry_ref.at[1, jb], first_tile, tiles_per_seg, newv_ref)
    cwg = cwg_ref[...]
    cwv = cwv_ref[...]
    for c in range(act_ref.shape[0]):
        lanes = slice(c * LANES, (c + 1) * LANES)
        for seg in range(n_seg):
            for q in range(CONV_PHASES):
                g = _slab_conv_phase(upg_ref, c, seg, n_seg, q, cwg[:, lanes])
                val = _slab_conv_phase(upv_ref, c, seg, n_seg, q, cwv[:, lanes])
                _phase_store(act_ref, c, seg, n_seg, q, g * jax.nn.sigmoid(g) * val)
    act = jnp.concatenate([act_ref[c] for c in range(act_ref.shape[0])], axis=-1)
    out_ref[...] += _dot(act.astype(_BF16), wd_ref[...])

    h = h_ref[...]
    _slab_store(upg_ref, _dot(h, wg_ref[...]), n_seg)
    _slab_store(upv_ref, _dot(h, wv_ref[...]), n_seg)

    if final_norm:
        @pl.when(jnp.logical_and(jb == n_ff_tiles - 1, s > 0))
        def _():
            y = out_ref[...]
            out_ref[...] = y * _rms_scale(y) * fg_ref[...]


def _ffn_call(x, state, tiles_per_seg, final_norm, n2_g, ffn_up, ffn_conv_w, ffn_down, final_g):
    rows, d_model = x.shape
    n_batch = state.shape[0]
    d_ff = ffn_down.shape[0]
    assert d_ff % FF_TILE == 0
    tm = min(FFN_ROW_TILE, rows)
    assert rows % tm == 0
    ni, nj = rows // tm, d_ff // FF_TILE
    segs_per_tile = _segments_per_tile(n_batch, tiles_per_seg, ni)
    assert nj >= 2
    n_items = ni * nj

    def item_a(s):
        t = jnp.minimum(s, n_items - 1)
        return t // nj, t % nj

    def item_b(s):
        t = jnp.maximum(s - 1, 0)
        return t // nj, t % nj

    def spec_a(shape, fn, **kw):
        return pl.BlockSpec(shape, lambda s: fn(*item_a(s)), **kw)

    def spec_b(shape, fn):
        return pl.BlockSpec(shape, lambda s: fn(*item_b(s)))

    full_vec = pl.BlockSpec((1, d_model), lambda s: (0, 0))
    state_shape = (segs_per_tile, CONV_K - 1, FF_TILE)
    in_specs = [
        spec_a((tm, d_model), lambda i, j: (i, 0), pipeline_mode=pl.Buffered(1)),
        spec_b(state_shape, lambda i, j: (i // tiles_per_seg, 0, j)),
        spec_b(state_shape, lambda i, j: (i // tiles_per_seg, 0, nj + j)),
        full_vec,
        spec_a((d_model, FF_TILE), lambda i, j: (0, j)),
        spec_a((d_model, FF_TILE), lambda i, j: (0, nj + j)),
        spec_b((CONV_K, FF_TILE), lambda i, j: (0, j)),
        spec_b((CONV_K, FF_TILE), lambda i, j: (0, nj + j)),
        spec_b((FF_TILE, d_model), lambda i, j: (j, 0)),
        full_vec,
    ]
    assert (tm // segs_per_tile) % (CONV_PHASES * SUBLANES) == 0
    slab_shape = (FF_TILE // LANES, tm + segs_per_tile * HEAD_ROWS, LANES)
    half_state = jax.ShapeDtypeStruct((ni * segs_per_tile, CONV_K - 1, d_ff), _F32)
    new_state_spec = spec_b(state_shape, lambda i, j: (i, 0, j))
    out_shape = [jax.ShapeDtypeStruct((rows, d_model), _F32), half_state, half_state]
    out_specs = [spec_b((tm, d_model), lambda i, j: (i, 0)), new_state_spec, new_state_spec]
    scratch = [pltpu.VMEM((tm, d_model), _BF16),
               pltpu.VMEM(slab_shape, _F32), pltpu.VMEM(slab_shape, _F32),
               pltpu.VMEM((FF_TILE // LANES, tm, LANES), _F32),
               pltpu.VMEM((2, nj, CONV_K - 1, FF_TILE), _F32)]
    y, new_g, new_v = pl.pallas_call(
        functools.partial(_ffn_kernel, n_ff_tiles=nj, tiles_per_seg=tiles_per_seg,
                          final_norm=final_norm),
        grid=(n_items + 1,), in_specs=in_specs, out_specs=out_specs, out_shape=out_shape,
        scratch_shapes=scratch, name="ffn",
        compiler_params=pltpu.CompilerParams(
            dimension_semantics=("arbitrary",),
            vmem_limit_bytes=VMEM_LIMIT_BYTES),
    )(x, state, state, n2_g, ffn_up, ffn_up, ffn_conv_w, ffn_conv_w, ffn_down, final_g)
    return y, jnp.concatenate([_last_tile_state(new_g, n_batch),
                               _last_tile_state(new_v, n_batch)], axis=-1)


def _spatial_operands(ws, bs, t):
    blk = jnp.arange(GMLP_CHUNK) // CHUNK
    mask = blk[:, None] >= blk[None, :]
    wm = jnp.where(mask[None], ws, 0.0)[:, :t, :t].astype(_BF16)
    bias = jnp.repeat(jnp.transpose(bs)[:t], HD_A, axis=1)
    return wm, bias


def kernel(x_prompt, x_sample, state_sconv, state_ffnconv, n1_g, w_in, vnorm_g, gmlp_ws, gmlp_bs,
           sconv_w, onorm_a_g, onorm_b_g, w_out, n2_g, ffn_up, ffn_conv_w, ffn_down, final_g):
    depth = w_in.shape[0]
    batch, seq, d_model = x_prompt.shape
    dec_batch, dec_seq, _ = x_sample.shape
    d_b = state_sconv.shape[-1]
    d_ff2 = state_ffnconv.shape[-1]
    assert seq % FFN_ROW_TILE == 0 and seq % MIX_ROW_TILE == 0 and MIX_ROW_TILE % GMLP_CHUNK == 0
    assert dec_seq <= GMLP_CHUNK

    xp = x_prompt.reshape(batch * seq, d_model)
    xs = x_sample.reshape(dec_batch * dec_seq, d_model)
    zero_sconv = jnp.zeros((batch, CONV_K - 1, d_b), _F32)
    zero_ffn = jnp.zeros((batch, CONV_K - 1, d_ff2), _F32)
    fg = final_g.reshape(1, d_model)

    p_sconv, p_ffn, s_v, s_sconv, s_ffn = [], [], [], [], []
    for l in range(depth):
        last = l == depth - 1
        wm_p, bias_p = _spatial_operands(gmlp_ws[l], gmlp_bs[l], GMLP_CHUNK)
        wm_s, bias_s = _spatial_operands(gmlp_ws[l], gmlp_bs[l], dec_seq)
        mixer_w = (n1_g[l].reshape(1, -1), w_in[l].astype(_BF16), vnorm_g[l].reshape(1, -1))
        mixer_w2 = (sconv_w[l], onorm_a_g[l].reshape(1, -1), onorm_b_g[l].reshape(1, -1),
                    w_out[l].astype(_BF16))
        ffn_w = (n2_g[l].reshape(1, -1), ffn_up[l].astype(_BF16), ffn_conv_w[l],
                 ffn_down[l].astype(_BF16), fg)

        xp, ps = _mixer_call(xp, zero_sconv, seq // MIX_ROW_TILE, False, *mixer_w, wm_p, bias_p,
                             *mixer_w2)
        xp, pf = _ffn_call(xp, zero_ffn, seq // FFN_ROW_TILE, last, *ffn_w)
        xs, ss, sv = _mixer_call(xs, state_sconv[l], 1, True, *mixer_w, wm_s, bias_s, *mixer_w2)
        xs, sf = _ffn_call(xs, state_ffnconv[l], 1, last, *ffn_w)
        p_sconv.append(ps)
        p_ffn.append(pf)
        s_v.append(sv.reshape(dec_batch, dec_seq, -1))
        s_sconv.append(ss)
        s_ffn.append(sf)

    return (xp.reshape(batch, seq, d_model), xs.reshape(dec_batch, dec_seq, d_model),
            jnp.stack(p_sconv), jnp.stack(p_ffn), jnp.stack(s_v), jnp.stack(s_sconv),
            jnp.stack(s_ffn))
```

```python
import functools
import math

import jax
import jax.numpy as jnp
from jax import lax
from jax.experimental import pallas as pl
from jax.experimental.pallas import tpu as pltpu

EPS = 1e-6
H_A = 8
HD_A = 128
CHUNK = 64
GMLP_CHUNK = 128
CONV_K = 3
SUBLANES = 8
LANES = 128
CONV_PHASES = 4
HEAD_ROWS = SUBLANES

MIX_ROW_TILE = 256
MIX_PIECE = 256
FFN_ROW_TILE = 1024
FF_TILE = 512
VMEM_LIMIT_BYTES = 58 * 1024 * 1024

_BF16 = jnp.bfloat16
_F32 = jnp.float32


def _dot(a, b):
    return jnp.dot(a, b, preferred_element_type=_F32)


def _gelu_tanh(x):
    c = math.sqrt(2.0 / math.pi)
    return 0.5 * x * (1.0 + jnp.tanh(c * (x + 0.044715 * (x * x * x))))


def _silu(x):
    hx = 0.5 * x
    return hx + hx * jnp.tanh(hx)


def _rms_scale(x):
    return lax.rsqrt(jnp.mean(x * x, axis=-1, keepdims=True) + EPS)


def _slab_seg_rows(slab_ref, n_seg):
    return slab_ref.shape[1] // n_seg - HEAD_ROWS


def _slab_store(slab_ref, val, n_seg):
    seg_rows = _slab_seg_rows(slab_ref, n_seg)
    for c in range(slab_ref.shape[0]):
        for s in range(n_seg):
            r0 = s * (HEAD_ROWS + seg_rows) + HEAD_ROWS
            slab_ref[c, r0:r0 + seg_rows, :] = val[s * seg_rows:(s + 1) * seg_rows,
                                                   c * LANES:(c + 1) * LANES]


def _slab_conv_prepare(slab_ref, state_ref, carry_ref, first_tile, tiles_per_seg, new_state_ref,
                       lane0=0):
    n_seg = state_ref.shape[0]
    seg_rows = _slab_seg_rows(slab_ref, n_seg)
    width = slab_ref.shape[0] * LANES
    for s in range(n_seg):
        if tiles_per_seg == 1:
            prev = state_ref[s, :, lane0:lane0 + width]
        else:
            prev = jnp.where(first_tile, state_ref[s, :, lane0:lane0 + width],
                             carry_ref[:, lane0:lane0 + width])
        r0 = s * (HEAD_ROWS + seg_rows) + HEAD_ROWS
        lasts = []
        for c in range(slab_ref.shape[0]):
            slab_ref[c, r0 - (CONV_K - 1):r0, :] = prev[:, c * LANES:(c + 1) * LANES]
            lasts.append(slab_ref[c, r0 + seg_rows - (CONV_K - 1):r0 + seg_rows, :])
        last = jnp.concatenate(lasts, axis=-1)
        new_state_ref[s, :, lane0:lane0 + width] = last
        if tiles_per_seg != 1:
            carry_ref[:, lane0:lane0 + width] = last


def _slab_conv_phase(slab_ref, c, seg, n_seg, q, w, part=0, n_parts=1):
    seg_rows = _slab_seg_rows(slab_ref, n_seg)
    part_rows = seg_rows // n_parts
    base = seg * (HEAD_ROWS + seg_rows) + HEAD_ROWS + part * part_rows + q
    n = part_rows // CONV_PHASES
    t0, t1, t2 = [slab_ref[c, pl.ds(base - d, n, stride=CONV_PHASES), :] for d in range(CONV_K)]
    return w[2:3] * t0 + w[1:2] * t1 + w[0:1] * t2


def _phase_store(dst_ref, c, seg, n_seg, q, val, part=0, n_parts=1):
    part_rows = dst_ref.shape[1] // n_seg // n_parts
    start = (seg * n_parts + part) * part_rows + q
    dst_ref[c, pl.ds(start, part_rows // CONV_PHASES, stride=CONV_PHASES), :] = val


def _segments_per_tile(n_batch, tiles_per_seg, n_tiles):
    segs = n_batch // n_tiles if tiles_per_seg == 1 else 1
    assert segs * n_tiles == n_batch * tiles_per_seg
    return segs


def _last_tile_state(per_tile_state, n_batch):
    _, k, c = per_tile_state.shape
    return per_tile_state.reshape(n_batch, -1, k, c)[:, -1]


def _mixer_kernel(x_ref, xprev_ref, state_ref, n1g_ref, win_ref, vng_ref, ws_ref, bias_ref,
                  cw_ref, ona_ref, onb_ref, wout_ref, *rest, tiles_per_seg, emit_v):
    if emit_v:
        out_ref, newstate_ref, v_ref, mix_ref, pslab_ref, cb_ref, carry_ref = rest
    else:
        out_ref, newstate_ref, mix_ref, pslab_ref, cb_ref, carry_ref = rest
        v_ref = None
    tm = x_ref.shape[0]
    d_a = vng_ref.shape[1]
    n_pieces = d_a // MIX_PIECE
    n_seg = state_ref.shape[0]
    slabs_per_piece = MIX_PIECE // LANES
    heads_per_piece = MIX_PIECE // HD_A
    s = pl.program_id(0)
    tile_a = jnp.minimum(s, pl.num_programs(0) - 2)
    first_tile = tile_a % tiles_per_seg == 0

    @pl.when(s == 0)
    def _():
        mix_ref[...] = jnp.zeros_like(mix_ref)

    mix_prev = mix_ref[...]
    n_out = wout_ref.shape[0]

    def out_proj(n):
        return _dot(mix_prev, wout_ref[n])

    x = x_ref[...]
    h = (x * _rms_scale(x) * n1g_ref[...]).astype(_BF16)
    g_rows = ws_ref.shape[1]

    def in_proj(j):
        return [_dot(h, win_ref[k * n_pieces + j]) for k in range(5)]

    def gate_inputs(j, pu, pv):
        cols = slice(j * MIX_PIECE, (j + 1) * MIX_PIECE)
        u = _gelu_tanh(pu)
        v = _gelu_tanh(pv)
        heads = [v[:, hh * HD_A:(hh + 1) * HD_A] for hh in range(heads_per_piece)]
        vn = jnp.concatenate([vh * _rms_scale(vh) for vh in heads], axis=-1) * vng_ref[:, cols]
        if emit_v:
            v_ref[:, cols] = vn
        return u, vn.astype(_BF16)

    def spatial(j, vb):
        bias = bias_ref[:, j * MIX_PIECE:(j + 1) * MIX_PIECE]
        z_chunks = []
        for c in range(tm // g_rows):
            zh = [_dot(ws_ref[j * heads_per_piece + hh],
                       vb[c * g_rows:(c + 1) * g_rows, hh * HD_A:(hh + 1) * HD_A])
                  for hh in range(heads_per_piece)]
            z_chunks.append(jnp.concatenate(zh, axis=-1) + bias)
        return jnp.concatenate(z_chunks, axis=0)

    def gated_conv(j, gb, gc, hin):
        slab = pslab_ref.at[j * slabs_per_piece:(j + 1) * slabs_per_piece]
        _slab_store(slab, gc * hin, n_seg)
        _slab_conv_prepare(slab, state_ref, carry_ref, first_tile, tiles_per_seg, newstate_ref,
                           lane0=j * MIX_PIECE)
        cw = cw_ref[:, j * MIX_PIECE:(j + 1) * MIX_PIECE]
        for c in range(slabs_per_piece):
            for seg in range(n_seg):
                for q in range(CONV_PHASES):
                    _phase_store(cb_ref, j * slabs_per_piece + c, seg, n_seg, q,
                                 _slab_conv_phase(slab, c, seg, n_seg, q,
                                                  cw[:, c * LANES:(c + 1) * LANES]))
        cb = jnp.concatenate([cb_ref[j * slabs_per_piece + c] for c in range(slabs_per_piece)],
                             axis=-1)
        return gb * cb

    deltas = [out_proj(n) for n in range(n_out // 2)]
    projs = [in_proj(0)]
    a_pieces, b_pieces = [], []
    for j in range(n_pieces):
        if j + 1 < n_pieces:
            projs.append(in_proj(j + 1))
        else:
            deltas += [out_proj(n) for n in range(n_out // 2, 3 * n_out // 4)]
        pu, pv, gb, gc, hin = projs[j]
        u, vb = gate_inputs(j, pu, pv)
        a_pieces.append(u * spatial(j, vb))
        b_pieces.append(gated_conv(j, gb, gc, hin))
    deltas += [out_proj(n) for n in range(3 * n_out // 4, n_out)]
    out_ref[...] = xprev_ref[...] + jnp.concatenate(deltas, axis=-1)

    a = jnp.concatenate(a_pieces, axis=-1)
    b = jnp.concatenate(b_pieces, axis=-1)
    mix = jnp.concatenate([a * _rms_scale(a) * ona_ref[...], b * _rms_scale(b) * onb_ref[...]],
                          axis=-1)
    mix_ref[...] = mix.astype(_BF16)


def _mixer_call(x, state, tiles_per_seg, emit_v, layer, n1_g, w_in_pieces, vnorm_g, ws_masked,
                bias_full, sconv_w, onorm_a_g, onorm_b_g, w_out_pieces):
    rows, d_model = x.shape
    n_batch, _, d_b = state.shape
    d_a = vnorm_g.shape[1]
    assert d_a == d_b and d_a % MIX_PIECE == 0
    assert w_in_pieces.shape[1:] == (5 * d_a // MIX_PIECE, d_model, MIX_PIECE)
    assert w_out_pieces.shape[1:] == (d_model // MIX_PIECE, 2 * d_a, MIX_PIECE)
    tm = min(MIX_ROW_TILE, rows)
    assert rows % tm == 0
    ni = rows // tm
    segs_per_tile = _segments_per_tile(n_batch, tiles_per_seg, ni)
    assert (tm // segs_per_tile) % (CONV_PHASES * SUBLANES) == 0
    g_rows = ws_masked.shape[1]
    assert (tm // segs_per_tile) % g_rows == 0

    def resident(shape):
        return pl.BlockSpec(shape, lambda s: (0,) * len(shape), pipeline_mode=pl.Buffered(1))

    def layer_resident(shape):
        return pl.BlockSpec((None,) + shape, lambda s: (layer,) + (0,) * len(shape),
                            pipeline_mode=pl.Buffered(1))

    def tile_a(s):
        return jnp.minimum(s, ni - 1)

    def tile_b(s):
        return jnp.maximum(s - 1, 0)

    state_shape = (segs_per_tile, CONV_K - 1, d_b)
    in_specs = [
        pl.BlockSpec((tm, d_model), lambda s: (tile_a(s), 0)),
        pl.BlockSpec((tm, d_model), lambda s: (tile_b(s), 0)),
        pl.BlockSpec(state_shape, lambda s: (tile_a(s) // tiles_per_seg, 0, 0)),
        resident((1, d_model)), layer_resident(w_in_pieces.shape[1:]), resident((1, d_a)),
        resident(ws_masked.shape), resident(bias_full.shape), resident(sconv_w.shape),
        resident((1, d_a)), resident((1, d_b)), layer_resident(w_out_pieces.shape[1:]),
    ]
    out_shape = [jax.ShapeDtypeStruct((rows, d_model), _F32),
                 jax.ShapeDtypeStruct((ni * segs_per_tile, CONV_K - 1, d_b), _F32)]
    out_specs = [pl.BlockSpec((tm, d_model), lambda s: (tile_b(s), 0)),
                 pl.BlockSpec(state_shape, lambda s: (tile_a(s), 0, 0))]
    if emit_v:
        out_shape.append(jax.ShapeDtypeStruct((rows, d_a), _F32))
        out_specs.append(pl.BlockSpec((tm, d_a), lambda s: (tile_a(s), 0)))
    scratch = [
        pltpu.VMEM((tm, 2 * d_a), _BF16),
        pltpu.VMEM((d_b // LANES, tm + segs_per_tile * HEAD_ROWS, LANES), _F32),
        pltpu.VMEM((d_b // LANES, tm, LANES), _F32),
        pltpu.VMEM((CONV_K - 1, d_b), _F32),
    ]
    outs = pl.pallas_call(
        functools.partial(_mixer_kernel, tiles_per_seg=tiles_per_seg, emit_v=emit_v),
        grid=(ni + 1,), in_specs=in_specs, out_specs=out_specs, out_shape=out_shape,
        scratch_shapes=scratch, name="mixer",
        compiler_params=pltpu.CompilerParams(
            dimension_semantics=("arbitrary",),
            vmem_limit_bytes=VMEM_LIMIT_BYTES),
    )(x, x, state, n1_g, w_in_pieces, vnorm_g, ws_masked, bias_full, sconv_w, onorm_a_g,
      onorm_b_g, w_out_pieces)
    return (outs[0], _last_tile_state(outs[1], n_batch)) + tuple(outs[2:])


def _ffn_kernel(x_ref, stg_ref, stv_ref, n2g_ref, wg_ref, wv_ref, cwg_ref, cwv_ref, wd_ref,
                fg_ref, out_ref, newg_ref, newv_ref, h_ref, upa_ref, upb_ref, act_ref,
                carry_ref, *, n_ff_tiles, tiles_per_seg, final_norm):
    s = pl.program_id(0)
    n_items = pl.num_programs(0) - 1
    ja = s % n_ff_tiles
    item_b = jnp.maximum(s - 1, 0)
    ib = item_b // n_ff_tiles
    jb = item_b % n_ff_tiles

    @pl.when(s == 0)
    def _():
        upb_ref[...] = jnp.zeros_like(upb_ref)

    @pl.when(jnp.logical_and(ja == 0, s < n_items))
    def _():
        x = x_ref[...]
        h_ref[...] = (x * _rms_scale(x) * n2g_ref[...]).astype(_BF16)

    @pl.when(jb == 0)
    def _():
        out_ref[...] = x_ref[...]

    n_seg = stg_ref.shape[0]
    first_tile = ib % tiles_per_seg == 0

    def stages(up_new, up_old):
        h = h_ref[...]
        upg, upv = up_old.at[0], up_old.at[1]
        _slab_conv_prepare(upg, stg_ref, carry_ref.at[0, jb], first_tile, tiles_per_seg, newg_ref)
        _slab_conv_prepare(upv, stv_ref, carry_ref.at[1, jb], first_tile, tiles_per_seg, newv_ref)
        cwg = cwg_ref[...]
        cwv = cwv_ref[...]
        tm = out_ref.shape[0]
        segs_of_half = [range(n_seg)] * 2 if n_seg == 1 else [range(n_seg // 2),
                                                               range(n_seg // 2, n_seg)]
        n_parts = 2 if n_seg == 1 else 1

        def down_half(r):
            for c in range(act_ref.shape[0]):
                lanes = slice(c * LANES, (c + 1) * LANES)
                for seg in segs_of_half[r]:
                    part = r if n_seg == 1 else 0
                    for q in range(CONV_PHASES):
                        g = _slab_conv_phase(upg, c, seg, n_seg, q, cwg[:, lanes], part, n_parts)
                        val = _slab_conv_phase(upv, c, seg, n_seg, q, cwv[:, lanes], part,
                                               n_parts)
                        _phase_store(act_ref, c, seg, n_seg, q, _silu(g) * val, part, n_parts)
            rows = slice(r * tm // 2, (r + 1) * tm // 2)
            act = jnp.concatenate([act_ref[c, rows, :] for c in range(act_ref.shape[0])], axis=-1)
            out_ref[rows, :] += _dot(act.astype(_BF16), wd_ref[...])

        _slab_store(up_new.at[0], _dot(h, wg_ref[...]), n_seg)
        down_half(0)
        _slab_store(up_new.at[1], _dot(h, wv_ref[...]), n_seg)
        down_half(1)

    @pl.when(s % 2 == 0)
    def _():
        stages(upa_ref, upb_ref)

    @pl.when(s % 2 == 1)
    def _():
        stages(upb_ref, upa_ref)

    if final_norm:
        @pl.when(jnp.logical_and(jb == n_ff_tiles - 1, s > 0))
        def _():
            y = out_ref[...]
            out_ref[...] = y * _rms_scale(y) * fg_ref[...]


def _ffn_call(x, state, tiles_per_seg, final_norm, layer, n2_g, ffn_up_tiles, ffn_conv_w, ffn_down,
              final_g):
    rows, d_model = x.shape
    n_batch = state.shape[0]
    d_ff = ffn_down.shape[1]
    assert ffn_up_tiles.shape[1:] == (2 * d_ff // FF_TILE, d_model, FF_TILE)
    assert d_ff % FF_TILE == 0
    tm = min(FFN_ROW_TILE, rows)
    assert rows % tm == 0
    ni, nj = rows // tm, d_ff // FF_TILE
    segs_per_tile = _segments_per_tile(n_batch, tiles_per_seg, ni)
    assert nj >= 2
    n_items = ni * nj

    def item_a(s):
        t = jnp.minimum(s, n_items - 1)
        return t // nj, t % nj

    def item_b(s):
        t = jnp.maximum(s - 1, 0)
        return t // nj, t % nj

    def spec_a(shape, fn, **kw):
        return pl.BlockSpec(shape, lambda s: fn(*item_a(s)), **kw)

    def spec_b(shape, fn):
        return pl.BlockSpec(shape, lambda s: fn(*item_b(s)))

    full_vec = pl.BlockSpec((1, d_model), lambda s: (0, 0))
    state_shape = (segs_per_tile, CONV_K - 1, FF_TILE)
    in_specs = [
        spec_a((tm, d_model), lambda i, j: (i, 0), pipeline_mode=pl.Buffered(1)),
        spec_b(state_shape, lambda i, j: (i // tiles_per_seg, 0, j)),
        spec_b(state_shape, lambda i, j: (i // tiles_per_seg, 0, nj + j)),
        full_vec,
        spec_a((None, None, d_model, FF_TILE), lambda i, j: (layer, j, 0, 0)),
        spec_a((None, None, d_model, FF_TILE), lambda i, j: (layer, nj + j, 0, 0)),
        spec_b((CONV_K, FF_TILE), lambda i, j: (0, j)),
        spec_b((CONV_K, FF_TILE), lambda i, j: (0, nj + j)),
        spec_b((None, FF_TILE, d_model), lambda i, j: (layer, j, 0)),
        full_vec,
    ]
    assert (tm // segs_per_tile) % (CONV_PHASES * SUBLANES) == 0
    slab_shape = (FF_TILE // LANES, tm + segs_per_tile * HEAD_ROWS, LANES)
    half_state = jax.ShapeDtypeStruct((ni * segs_per_tile, CONV_K - 1, d_ff), _F32)
    new_state_spec = spec_b(state_shape, lambda i, j: (i, 0, j))
    out_shape = [jax.ShapeDtypeStruct((rows, d_model), _F32), half_state, half_state]
    out_specs = [spec_b((tm, d_model), lambda i, j: (i, 0)), new_state_spec, new_state_spec]
    scratch = [pltpu.VMEM((tm, d_model), _BF16),
               pltpu.VMEM((2,) + slab_shape, _F32), pltpu.VMEM((2,) + slab_shape, _F32),
               pltpu.VMEM((FF_TILE // LANES, tm, LANES), _F32),
               pltpu.VMEM((2, nj, CONV_K - 1, FF_TILE), _F32)]
    y, new_g, new_v = pl.pallas_call(
        functools.partial(_ffn_kernel, n_ff_tiles=nj, tiles_per_seg=tiles_per_seg,
                          final_norm=final_norm),
        grid=(n_items + 1,), in_specs=in_specs, out_specs=out_specs, out_shape=out_shape,
        scratch_shapes=scratch, name="ffn",
        compiler_params=pltpu.CompilerParams(
            dimension_semantics=("arbitrary",),
            vmem_limit_bytes=VMEM_LIMIT_BYTES),
    )(x, state, state, n2_g, ffn_up_tiles, ffn_up_tiles, ffn_conv_w, ffn_conv_w, ffn_down, final_g)
    return y, jnp.concatenate([_last_tile_state(new_g, n_batch),
                               _last_tile_state(new_v, n_batch)], axis=-1)


def _spatial_operands(ws, bs, t):
    blk = jnp.arange(GMLP_CHUNK) // CHUNK
    mask = blk[:, None] >= blk[None, :]
    wm = jnp.where(mask[None], ws, 0.0)[:, :t, :t].astype(_BF16)
    bias = jnp.repeat(jnp.transpose(bs)[:t], HD_A, axis=1)
    return wm, bias


def _column_pieces(w, width):
    depth, k, n = w.shape
    return jnp.transpose(w.astype(_BF16).reshape(depth, k, n // width, width), (0, 2, 1, 3))


def kernel(x_prompt, x_sample, state_sconv, state_ffnconv, n1_g, w_in, vnorm_g, gmlp_ws, gmlp_bs,
           sconv_w, onorm_a_g, onorm_b_g, w_out, n2_g, ffn_up, ffn_conv_w, ffn_down, final_g):
    depth = w_in.shape[0]
    batch, seq, d_model = x_prompt.shape
    dec_batch, dec_seq, _ = x_sample.shape
    d_b = state_sconv.shape[-1]
    d_ff2 = state_ffnconv.shape[-1]
    assert seq % FFN_ROW_TILE == 0 and seq % MIX_ROW_TILE == 0 and MIX_ROW_TILE % GMLP_CHUNK == 0
    assert dec_seq <= GMLP_CHUNK

    xp = x_prompt.reshape(batch * seq, d_model)
    xs = x_sample.reshape(dec_batch * dec_seq, d_model)
    zero_sconv = jnp.zeros((batch, CONV_K - 1, d_b), _F32)
    zero_ffn = jnp.zeros((batch, CONV_K - 1, d_ff2), _F32)
    fg = final_g.reshape(1, d_model)
    w_in_pieces = _column_pieces(w_in, MIX_PIECE)
    w_out_pieces = _column_pieces(w_out, MIX_PIECE)
    ffn_up_tiles = _column_pieces(ffn_up, FF_TILE)
    ffn_down_b = ffn_down.astype(_BF16)

    p_sconv, p_ffn, s_v, s_sconv, s_ffn = [], [], [], [], []
    for l in range(depth):
        last = l == depth - 1
        wm_p, bias_p = _spatial_operands(gmlp_ws[l], gmlp_bs[l], GMLP_CHUNK)
        wm_s, bias_s = _spatial_operands(gmlp_ws[l], gmlp_bs[l], dec_seq)
        mixer_w = (l, n1_g[l].reshape(1, -1), w_in_pieces, vnorm_g[l].reshape(1, -1))
        mixer_w2 = (sconv_w[l], onorm_a_g[l].reshape(1, -1), onorm_b_g[l].reshape(1, -1),
                    w_out_pieces)
        ffn_w = (l, n2_g[l].reshape(1, -1), ffn_up_tiles, ffn_conv_w[l], ffn_down_b, fg)

        xp, ps = _mixer_call(xp, zero_sconv, seq // MIX_ROW_TILE, False, *mixer_w, wm_p, bias_p,
                             *mixer_w2)
        xp, pf = _ffn_call(xp, zero_ffn, seq // FFN_ROW_TILE, last, *ffn_w)
        xs, ss, sv = _mixer_call(xs, state_sconv[l], 1, True, *mixer_w, wm_s, bias_s, *mixer_w2)
        xs, sf = _ffn_call(xs, state_ffnconv[l], 1, last, *ffn_w)
        p_sconv.append(ps)
        p_ffn.append(pf)
        s_v.append(sv.reshape(dec_batch, dec_seq, -1))
        s_sconv.append(ss)
        s_ffn.append(sf)

    return (xp.reshape(batch, seq, d_model), xs.reshape(dec_batch, dec_seq, d_model),
            jnp.stack(p_sconv), jnp.stack(p_ffn), jnp.stack(s_v), jnp.stack(s_sconv),
            jnp.stack(s_ffn))
```

```python
import functools
import math

import jax
import jax.numpy as jnp
from jax import lax
from jax.experimental import pallas as pl
from jax.experimental.pallas import tpu as pltpu

EPS = 1e-6
H_A = 8
HD_A = 128
CHUNK = 64
GMLP_CHUNK = 128
CONV_K = 3
SUBLANES = 8
LANES = 128
CONV_PHASES = 4
HEAD_ROWS = SUBLANES

MIX_ROW_TILE = 256
MIX_PIECE = 256
FFN_ROW_TILE = 1024
FF_TILE = 512
VMEM_LIMIT_BYTES = 58 * 1024 * 1024

_BF16 = jnp.bfloat16
_F32 = jnp.float32


def _dot(a, b):
    return jnp.dot(a, b, preferred_element_type=_F32)


def _gelu_tanh(x):
    c = math.sqrt(2.0 / math.pi)
    return 0.5 * x * (1.0 + jnp.tanh(c * (x + 0.044715 * (x * x * x))))


def _silu(x):
    hx = 0.5 * x
    return hx + hx * jnp.tanh(hx)


def _rms_scale(x):
    return lax.rsqrt(jnp.mean(x * x, axis=-1, keepdims=True) + EPS)


def _slab_seg_rows(slab_ref, n_seg):
    return slab_ref.shape[1] // n_seg - HEAD_ROWS


def _slab_store(slab_ref, val, n_seg):
    seg_rows = _slab_seg_rows(slab_ref, n_seg)
    for c in range(slab_ref.shape[0]):
        for s in range(n_seg):
            r0 = s * (HEAD_ROWS + seg_rows) + HEAD_ROWS
            slab_ref[c, r0:r0 + seg_rows, :] = val[s * seg_rows:(s + 1) * seg_rows,
                                                   c * LANES:(c + 1) * LANES]


def _slab_conv_prepare(slab_ref, state_ref, carry_ref, first_tile, tiles_per_seg, new_state_ref,
                       lane0=0):
    n_seg = state_ref.shape[0]
    seg_rows = _slab_seg_rows(slab_ref, n_seg)
    width = slab_ref.shape[0] * LANES
    for s in range(n_seg):
        if tiles_per_seg == 1:
            prev = state_ref[s, :, lane0:lane0 + width]
        else:
            prev = jnp.where(first_tile, state_ref[s, :, lane0:lane0 + width],
                             carry_ref[:, lane0:lane0 + width])
        r0 = s * (HEAD_ROWS + seg_rows) + HEAD_ROWS
        lasts = []
        for c in range(slab_ref.shape[0]):
            slab_ref[c, r0 - (CONV_K - 1):r0, :] = prev[:, c * LANES:(c + 1) * LANES]
            lasts.append(slab_ref[c, r0 + seg_rows - (CONV_K - 1):r0 + seg_rows, :])
        last = jnp.concatenate(lasts, axis=-1)
        new_state_ref[s, :, lane0:lane0 + width] = last
        if tiles_per_seg != 1:
            carry_ref[:, lane0:lane0 + width] = last


def _slab_conv_phases(slab_ref, c, seg, n_seg, w, part=0, n_parts=1):
    seg_rows = _slab_seg_rows(slab_ref, n_seg)
    part_rows = seg_rows // n_parts
    base = seg * (HEAD_ROWS + seg_rows) + HEAD_ROWS + part * part_rows - (CONV_K - 1)
    n = part_rows // CONV_PHASES
    rows = [slab_ref[c, pl.ds(base + m, n, stride=CONV_PHASES), :]
            for m in range(CONV_PHASES + CONV_K - 1)]
    return [w[2:3] * rows[q + 2] + w[1:2] * rows[q + 1] + w[0:1] * rows[q]
            for q in range(CONV_PHASES)]


def _phase_store(dst_ref, c, seg, n_seg, q, val, part=0, n_parts=1):
    part_rows = dst_ref.shape[1] // n_seg // n_parts
    start = (seg * n_parts + part) * part_rows + q
    dst_ref[c, pl.ds(start, part_rows // CONV_PHASES, stride=CONV_PHASES), :] = val


def _segments_per_tile(n_batch, tiles_per_seg, n_tiles):
    segs = n_batch // n_tiles if tiles_per_seg == 1 else 1
    assert segs * n_tiles == n_batch * tiles_per_seg
    return segs


def _last_tile_state(per_tile_state, n_batch):
    _, k, c = per_tile_state.shape
    return per_tile_state.reshape(n_batch, -1, k, c)[:, -1]


def _mixer_kernel(x_ref, xprev_ref, state_ref, n1g_ref, win_ref, vng_ref, ws_ref, bias_ref,
                  cw_ref, ona_ref, onb_ref, wout_ref, *rest, tiles_per_seg, emit_v):
    if emit_v:
        out_ref, newstate_ref, v_ref, mix_ref, pslab_ref, cb_ref, carry_ref = rest
    else:
        out_ref, newstate_ref, mix_ref, pslab_ref, cb_ref, carry_ref = rest
        v_ref = None
    tm = x_ref.shape[0]
    d_a = vng_ref.shape[1]
    n_pieces = d_a // MIX_PIECE
    n_seg = state_ref.shape[0]
    slabs_per_piece = MIX_PIECE // LANES
    heads_per_piece = MIX_PIECE // HD_A
    s = pl.program_id(0)
    tile_a = jnp.minimum(s, pl.num_programs(0) - 2)
    first_tile = tile_a % tiles_per_seg == 0

    @pl.when(s == 0)
    def _():
        mix_ref[...] = jnp.zeros_like(mix_ref)

    mix_prev = mix_ref[...]
    n_out = wout_ref.shape[0]

    def out_proj(n):
        return _dot(mix_prev, wout_ref[n])

    x = x_ref[...]
    h = (x * _rms_scale(x) * n1g_ref[...]).astype(_BF16)
    g_rows = ws_ref.shape[1]

    def in_proj(j):
        return [_dot(h, win_ref[k * n_pieces + j]) for k in range(5)]

    def gate_inputs(j, pu, pv):
        cols = slice(j * MIX_PIECE, (j + 1) * MIX_PIECE)
        u = _gelu_tanh(pu)
        v = _gelu_tanh(pv)
        heads = [v[:, hh * HD_A:(hh + 1) * HD_A] for hh in range(heads_per_piece)]
        vn = jnp.concatenate([vh * _rms_scale(vh) for vh in heads], axis=-1) * vng_ref[:, cols]
        if emit_v:
            v_ref[:, cols] = vn
        return u, vn.astype(_BF16)

    def spatial(j, vb):
        bias = bias_ref[:, j * MIX_PIECE:(j + 1) * MIX_PIECE]
        z_chunks = []
        for c in range(tm // g_rows):
            zh = [_dot(ws_ref[j * heads_per_piece + hh],
                       vb[c * g_rows:(c + 1) * g_rows, hh * HD_A:(hh + 1) * HD_A])
                  for hh in range(heads_per_piece)]
            z_chunks.append(jnp.concatenate(zh, axis=-1) + bias)
        return jnp.concatenate(z_chunks, axis=0)

    def gated_conv(j, gb, gc, hin):
        slab = pslab_ref.at[j * slabs_per_piece:(j + 1) * slabs_per_piece]
        _slab_store(slab, gc * hin, n_seg)
        _slab_conv_prepare(slab, state_ref, carry_ref, first_tile, tiles_per_seg, newstate_ref,
                           lane0=j * MIX_PIECE)
        cw = cw_ref[:, j * MIX_PIECE:(j + 1) * MIX_PIECE]
        for c in range(slabs_per_piece):
            for seg in range(n_seg):
                phases = _slab_conv_phases(slab, c, seg, n_seg, cw[:, c * LANES:(c + 1) * LANES])
                for q in range(CONV_PHASES):
                    _phase_store(cb_ref, j * slabs_per_piece + c, seg, n_seg, q, phases[q])
        cb = jnp.concatenate([cb_ref[j * slabs_per_piece + c] for c in range(slabs_per_piece)],
                             axis=-1)
        return gb * cb

    deltas = [out_proj(n) for n in range(n_out // 2)]
    projs = [in_proj(0)]
    a_pieces, b_pieces = [], []
    for j in range(n_pieces):
        if j + 1 < n_pieces:
            projs.append(in_proj(j + 1))
        else:
            deltas += [out_proj(n) for n in range(n_out // 2, 3 * n_out // 4)]
        pu, pv, gb, gc, hin = projs[j]
        u, vb = gate_inputs(j, pu, pv)
        a_pieces.append(u * spatial(j, vb))
        b_pieces.append(gated_conv(j, gb, gc, hin))
    deltas += [out_proj(n) for n in range(3 * n_out // 4, n_out)]
    out_ref[...] = xprev_ref[...] + jnp.concatenate(deltas, axis=-1)

    a = jnp.concatenate(a_pieces, axis=-1)
    b = jnp.concatenate(b_pieces, axis=-1)
    mix = jnp.concatenate([a * _rms_scale(a) * ona_ref[...], b * _rms_scale(b) * onb_ref[...]],
                          axis=-1)
    mix_ref[...] = mix.astype(_BF16)


def _mixer_call(x, state, tiles_per_seg, emit_v, layer, n1_g, w_in_pieces, vnorm_g, ws_masked,
                bias_full, sconv_w, onorm_a_g, onorm_b_g, w_out_pieces):
    rows, d_model = x.shape
    n_batch, _, d_b = state.shape
    d_a = vnorm_g.shape[1]
    assert d_a == d_b and d_a % MIX_PIECE == 0
    assert w_in_pieces.shape[1:] == (5 * d_a // MIX_PIECE, d_model, MIX_PIECE)
    assert w_out_pieces.shape[1:] == (d_model // MIX_PIECE, 2 * d_a, MIX_PIECE)
    tm = min(MIX_ROW_TILE, rows)
    assert rows % tm == 0
    ni = rows // tm
    segs_per_tile = _segments_per_tile(n_batch, tiles_per_seg, ni)
    assert (tm // segs_per_tile) % (CONV_PHASES * SUBLANES) == 0
    g_rows = ws_masked.shape[1]
    assert (tm // segs_per_tile) % g_rows == 0

    def resident(shape):
        return pl.BlockSpec(shape, lambda s: (0,) * len(shape), pipeline_mode=pl.Buffered(1))

    def layer_resident(shape):
        return pl.BlockSpec((None,) + shape, lambda s: (layer,) + (0,) * len(shape),
                            pipeline_mode=pl.Buffered(1))

    def tile_a(s):
        return jnp.minimum(s, ni - 1)

    def tile_b(s):
        return jnp.maximum(s - 1, 0)

    state_shape = (segs_per_tile, CONV_K - 1, d_b)
    in_specs = [
        pl.BlockSpec((tm, d_model), lambda s: (tile_a(s), 0)),
        pl.BlockSpec((tm, d_model), lambda s: (tile_b(s), 0)),
        pl.BlockSpec(state_shape, lambda s: (tile_a(s) // tiles_per_seg, 0, 0)),
        resident((1, d_model)), layer_resident(w_in_pieces.shape[1:]), resident((1, d_a)),
        resident(ws_masked.shape), resident(bias_full.shape), resident(sconv_w.shape),
        resident((1, d_a)), resident((1, d_b)), layer_resident(w_out_pieces.shape[1:]),
    ]
    out_shape = [jax.ShapeDtypeStruct((rows, d_model), _F32),
                 jax.ShapeDtypeStruct((ni * segs_per_tile, CONV_K - 1, d_b), _F32)]
    out_specs = [pl.BlockSpec((tm, d_model), lambda s: (tile_b(s), 0)),
                 pl.BlockSpec(state_shape, lambda s: (tile_a(s), 0, 0))]
    if emit_v:
        out_shape.append(jax.ShapeDtypeStruct((rows, d_a), _F32))
        out_specs.append(pl.BlockSpec((tm, d_a), lambda s: (tile_a(s), 0)))
    scratch = [
        pltpu.VMEM((tm, 2 * d_a), _BF16),
        pltpu.VMEM((d_b // LANES, tm + segs_per_tile * HEAD_ROWS, LANES), _F32),
        pltpu.VMEM((d_b // LANES, tm, LANES), _F32),
        pltpu.VMEM((CONV_K - 1, d_b), _F32),
    ]
    outs = pl.pallas_call(
        functools.partial(_mixer_kernel, tiles_per_seg=tiles_per_seg, emit_v=emit_v),
        grid=(ni + 1,), in_specs=in_specs, out_specs=out_specs, out_shape=out_shape,
        scratch_shapes=scratch, name="mixer",
        compiler_params=pltpu.CompilerParams(
            dimension_semantics=("arbitrary",),
            vmem_limit_bytes=VMEM_LIMIT_BYTES),
    )(x, x, state, n1_g, w_in_pieces, vnorm_g, ws_masked, bias_full, sconv_w, onorm_a_g,
      onorm_b_g, w_out_pieces)
    return (outs[0], _last_tile_state(outs[1], n_batch)) + tuple(outs[2:])


def _ffn_kernel(x_hbm, stg_ref, stv_ref, n2g_ref, wg_ref, wv_ref, cwg_ref, cwv_ref, wd_ref,
                fg_ref, out_ref, newg_ref, newv_ref, h_ref, upa_ref, upb_ref, act_ref,
                carry_ref, xbuf_ref, xsem, *, n_ff_tiles, tiles_per_seg, final_norm):
    s = pl.program_id(0)
    n_items = pl.num_programs(0) - 1
    tm = xbuf_ref.shape[0]
    n_tiles = x_hbm.shape[0] // tm
    ia = s // n_ff_tiles
    ja = s % n_ff_tiles
    item_b = jnp.maximum(s - 1, 0)
    ib = item_b // n_ff_tiles
    jb = item_b % n_ff_tiles

    @pl.when(s == 0)
    def _():
        upb_ref[...] = jnp.zeros_like(upb_ref)

    def x_copy(tile):
        rows = pl.ds(pl.multiple_of(tile * tm, tm), tm)
        return pltpu.make_async_copy(x_hbm.at[rows], xbuf_ref, xsem.at[0])

    @pl.when(s == 0)
    def _():
        x_copy(0).start()

    @pl.when(jnp.logical_and(ja == 2, ia + 1 < n_tiles))
    def _():
        x_copy(ia + 1).start()

    @pl.when(jnp.logical_and(ja == 0, s < n_items))
    def _():
        x_copy(ia).wait()
        x = xbuf_ref[...]
        h_ref[...] = (x * _rms_scale(x) * n2g_ref[...]).astype(_BF16)

    @pl.when(jb == 0)
    def _():
        out_ref[...] = xbuf_ref[...]

    n_seg = stg_ref.shape[0]
    first_tile = ib % tiles_per_seg == 0

    def stages(up_new, up_old):
        h = h_ref[...]
        upg, upv = up_old.at[0], up_old.at[1]
        _slab_conv_prepare(upg, stg_ref, carry_ref.at[0, jb], first_tile, tiles_per_seg, newg_ref)
        _slab_conv_prepare(upv, stv_ref, carry_ref.at[1, jb], first_tile, tiles_per_seg, newv_ref)
        cwg = cwg_ref[...]
        cwv = cwv_ref[...]
        segs_of_half = [range(n_seg)] * 2 if n_seg == 1 else [range(n_seg // 2),
                                                               range(n_seg // 2, n_seg)]
        n_parts = 2 if n_seg == 1 else 1

        def down_half(r):
            for c in range(act_ref.shape[0]):
                lanes = slice(c * LANES, (c + 1) * LANES)
                for seg in segs_of_half[r]:
                    part = r if n_seg == 1 else 0
                    g = _slab_conv_phases(upg, c, seg, n_seg, cwg[:, lanes], part, n_parts)
                    val = _slab_conv_phases(upv, c, seg, n_seg, cwv[:, lanes], part, n_parts)
                    for q in range(CONV_PHASES):
                        _phase_store(act_ref, c, seg, n_seg, q, _silu(g[q]) * val[q], part,
                                     n_parts)
            rows = slice(r * tm // 2, (r + 1) * tm // 2)
            act = jnp.concatenate([act_ref[c, rows, :] for c in range(act_ref.shape[0])], axis=-1)
            out_ref[rows, :] += _dot(act.astype(_BF16), wd_ref[...])

        _slab_store(up_new.at[0], _dot(h, wg_ref[...]), n_seg)
        down_half(0)
        _slab_store(up_new.at[1], _dot(h, wv_ref[...]), n_seg)
        down_half(1)

    @pl.when(s % 2 == 0)
    def _():
        stages(upa_ref, upb_ref)

    @pl.when(s % 2 == 1)
    def _():
        stages(upb_ref, upa_ref)

    if final_norm:
        @pl.when(jnp.logical_and(jb == n_ff_tiles - 1, s > 0))
        def _():
            y = out_ref[...]
            out_ref[...] = y * _rms_scale(y) * fg_ref[...]


def _ffn_call(x, state, tiles_per_seg, final_norm, layer, n2_g, ffn_up, ffn_conv_w, ffn_down,
              final_g):
    rows, d_model = x.shape
    n_batch = state.shape[0]
    d_ff = ffn_down.shape[1]
    assert ffn_up.shape[1:] == (d_model, 2 * d_ff)
    assert d_ff % FF_TILE == 0
    tm = min(FFN_ROW_TILE, rows)
    assert rows % tm == 0
    ni, nj = rows // tm, d_ff // FF_TILE
    segs_per_tile = _segments_per_tile(n_batch, tiles_per_seg, ni)
    assert nj >= 3
    n_items = ni * nj

    def item_a(s):
        t = jnp.minimum(s, n_items - 1)
        return t // nj, t % nj

    def item_b(s):
        t = jnp.maximum(s - 1, 0)
        return t // nj, t % nj

    def spec_a(shape, fn, **kw):
        return pl.BlockSpec(shape, lambda s: fn(*item_a(s)), **kw)

    def spec_b(shape, fn):
        return pl.BlockSpec(shape, lambda s: fn(*item_b(s)))

    full_vec = pl.BlockSpec((1, d_model), lambda s: (0, 0))
    state_shape = (segs_per_tile, CONV_K - 1, FF_TILE)
    in_specs = [
        pl.BlockSpec(memory_space=pl.ANY),
        spec_b(state_shape, lambda i, j: (i // tiles_per_seg, 0, j)),
        spec_b(state_shape, lambda i, j: (i // tiles_per_seg, 0, nj + j)),
        full_vec,
        spec_a((None, d_model, FF_TILE), lambda i, j: (layer, 0, j)),
        spec_a((None, d_model, FF_TILE), lambda i, j: (layer, 0, nj + j)),
        spec_b((CONV_K, FF_TILE), lambda i, j: (0, j)),
        spec_b((CONV_K, FF_TILE), lambda i, j: (0, nj + j)),
        spec_b((None, FF_TILE, d_model), lambda i, j: (layer, j, 0)),
        full_vec,
    ]
    assert (tm // segs_per_tile) % (CONV_PHASES * SUBLANES) == 0
    slab_shape = (FF_TILE // LANES, tm + segs_per_tile * HEAD_ROWS, LANES)
    half_state = jax.ShapeDtypeStruct((ni * segs_per_tile, CONV_K - 1, d_ff), _F32)
    new_state_spec = spec_b(state_shape, lambda i, j: (i, 0, j))
    out_shape = [jax.ShapeDtypeStruct((rows, d_model), _F32), half_state, half_state]
    out_specs = [spec_b((tm, d_model), lambda i, j: (i, 0)), new_state_spec, new_state_spec]
    scratch = [pltpu.VMEM((tm, d_model), _BF16),
               pltpu.VMEM((2,) + slab_shape, _F32), pltpu.VMEM((2,) + slab_shape, _F32),
               pltpu.VMEM((FF_TILE // LANES, tm, LANES), _F32),
               pltpu.VMEM((2, nj, CONV_K - 1, FF_TILE), _F32),
               pltpu.VMEM((tm, d_model), _F32), pltpu.SemaphoreType.DMA((1,))]
    y, new_g, new_v = pl.pallas_call(
        functools.partial(_ffn_kernel, n_ff_tiles=nj, tiles_per_seg=tiles_per_seg,
                          final_norm=final_norm),
        grid=(n_items + 1,), in_specs=in_specs, out_specs=out_specs, out_shape=out_shape,
        scratch_shapes=scratch, name="ffn",
        compiler_params=pltpu.CompilerParams(
            dimension_semantics=("arbitrary",),
            vmem_limit_bytes=VMEM_LIMIT_BYTES),
    )(x, state, state, n2_g, ffn_up, ffn_up, ffn_conv_w, ffn_conv_w, ffn_down, final_g)
    return y, jnp.concatenate([_last_tile_state(new_g, n_batch),
                               _last_tile_state(new_v, n_batch)], axis=-1)


def _spatial_operands(ws, bs, t):
    blk = jnp.arange(GMLP_CHUNK) // CHUNK
    mask = blk[:, None] >= blk[None, :]
    wm = jnp.where(mask[None], ws, 0.0)[:, :t, :t].astype(_BF16)
    bias = jnp.repeat(jnp.transpose(bs)[:t], HD_A, axis=1)
    return wm, bias


def _column_pieces(w, width):
    depth, k, n = w.shape
    return jnp.transpose(w.astype(_BF16).reshape(depth, k, n // width, width), (0, 2, 1, 3))


def kernel(x_prompt, x_sample, state_sconv, state_ffnconv, n1_g, w_in, vnorm_g, gmlp_ws, gmlp_bs,
           sconv_w, onorm_a_g, onorm_b_g, w_out, n2_g, ffn_up, ffn_conv_w, ffn_down, final_g):
    depth = w_in.shape[0]
    batch, seq, d_model = x_prompt.shape
    dec_batch, dec_seq, _ = x_sample.shape
    d_b = state_sconv.shape[-1]
    d_ff2 = state_ffnconv.shape[-1]
    assert seq % FFN_ROW_TILE == 0 and seq % MIX_ROW_TILE == 0 and MIX_ROW_TILE % GMLP_CHUNK == 0
    assert dec_seq <= GMLP_CHUNK

    xp = x_prompt.reshape(batch * seq, d_model)
    xs = x_sample.reshape(dec_batch * dec_seq, d_model)
    zero_sconv = jnp.zeros((batch, CONV_K - 1, d_b), _F32)
    zero_ffn = jnp.zeros((batch, CONV_K - 1, d_ff2), _F32)
    fg = final_g.reshape(1, d_model)
    w_in_pieces = _column_pieces(w_in, MIX_PIECE)
    w_out_pieces = _column_pieces(w_out, MIX_PIECE)
    ffn_up_b = ffn_up.astype(_BF16)
    ffn_down_b = ffn_down.astype(_BF16)

    p_sconv, p_ffn, s_v, s_sconv, s_ffn = [], [], [], [], []
    for l in range(depth):
        last = l == depth - 1
        wm_p, bias_p = _spatial_operands(gmlp_ws[l], gmlp_bs[l], GMLP_CHUNK)
        wm_s, bias_s = _spatial_operands(gmlp_ws[l], gmlp_bs[l], dec_seq)
        mixer_w = (l, n1_g[l].reshape(1, -1), w_in_pieces, vnorm_g[l].reshape(1, -1))
        mixer_w2 = (sconv_w[l], onorm_a_g[l].reshape(1, -1), onorm_b_g[l].reshape(1, -1),
                    w_out_pieces)
        ffn_w = (l, n2_g[l].reshape(1, -1), ffn_up_b, ffn_conv_w[l], ffn_down_b, fg)

        xp, ps = _mixer_call(xp, zero_sconv, seq // MIX_ROW_TILE, False, *mixer_w, wm_p, bias_p,
                             *mixer_w2)
        xp, pf = _ffn_call(xp, zero_ffn, seq // FFN_ROW_TILE, last, *ffn_w)
        xs, ss, sv = _mixer_call(xs, state_sconv[l], 1, True, *mixer_w, wm_s, bias_s, *mixer_w2)
        xs, sf = _ffn_call(xs, state_ffnconv[l], 1, last, *ffn_w)
        p_sconv.append(ps)
        p_ffn.append(pf)
        s_v.append(sv.reshape(dec_batch, dec_seq, -1))
        s_sconv.append(ss)
        s_ffn.append(sf)

    return (xp.reshape(batch, seq, d_model), xs.reshape(dec_batch, dec_seq, d_model),
            jnp.stack(p_sconv), jnp.stack(p_ffn), jnp.stack(s_v), jnp.stack(s_sconv),
            jnp.stack(s_ffn))
```

```python
import functools
import math

import jax
import jax.numpy as jnp
from jax import lax
from jax.experimental import pallas as pl
from jax.experimental.pallas import tpu as pltpu

EPS = 1e-6
H_A = 8
HD_A = 128
CHUNK = 64
GMLP_CHUNK = 128
CONV_K = 3
SUBLANES = 8
LANES = 128
CONV_PHASES = 4
HEAD_ROWS = SUBLANES

MIX_ROW_TILE = 256
MIX_PIECE = 256
FFN_ROW_TILE = 1024
FF_TILE = 512
FFN_ROW_PARTS = 4
VMEM_LIMIT_BYTES = 58 * 1024 * 1024

_BF16 = jnp.bfloat16
_F32 = jnp.float32


def _dot(a, b):
    return jnp.dot(a, b, preferred_element_type=_F32)


def _gelu_tanh(x):
    c = math.sqrt(2.0 / math.pi)
    return 0.5 * x * (1.0 + jnp.tanh(c * (x + 0.044715 * (x * x * x))))


def _silu(x):
    hx = 0.5 * x
    return hx + hx * jnp.tanh(hx)


def _rms_scale(x):
    return lax.rsqrt(jnp.mean(x * x, axis=-1, keepdims=True) + EPS)


def _slab_seg_rows(slab_ref, n_seg):
    return slab_ref.shape[1] // n_seg - HEAD_ROWS


def _slab_store(slab_ref, val, n_seg):
    seg_rows = _slab_seg_rows(slab_ref, n_seg)
    for c in range(slab_ref.shape[0]):
        for s in range(n_seg):
            r0 = s * (HEAD_ROWS + seg_rows) + HEAD_ROWS
            slab_ref[c, r0:r0 + seg_rows, :] = val[s * seg_rows:(s + 1) * seg_rows,
                                                   c * LANES:(c + 1) * LANES]


def _slab_conv_prepare(slab_ref, state_ref, carry_ref, first_tile, tiles_per_seg, new_state_ref,
                       lane0=0):
    n_seg = state_ref.shape[0]
    seg_rows = _slab_seg_rows(slab_ref, n_seg)
    width = slab_ref.shape[0] * LANES
    for s in range(n_seg):
        if tiles_per_seg == 1:
            prev = state_ref[s, :, lane0:lane0 + width]
        else:
            prev = jnp.where(first_tile, state_ref[s, :, lane0:lane0 + width],
                             carry_ref[:, lane0:lane0 + width])
        r0 = s * (HEAD_ROWS + seg_rows) + HEAD_ROWS
        lasts = []
        for c in range(slab_ref.shape[0]):
            slab_ref[c, r0 - (CONV_K - 1):r0, :] = prev[:, c * LANES:(c + 1) * LANES]
            lasts.append(slab_ref[c, r0 + seg_rows - (CONV_K - 1):r0 + seg_rows, :])
        last = jnp.concatenate(lasts, axis=-1)
        new_state_ref[s, :, lane0:lane0 + width] = last
        if tiles_per_seg != 1:
            carry_ref[:, lane0:lane0 + width] = last


def _slab_conv_phases(slab_ref, c, seg, n_seg, w, part=0, n_parts=1):
    seg_rows = _slab_seg_rows(slab_ref, n_seg)
    part_rows = seg_rows // n_parts
    base = seg * (HEAD_ROWS + seg_rows) + HEAD_ROWS + part * part_rows - (CONV_K - 1)
    n = part_rows // CONV_PHASES
    rows = [slab_ref[c, pl.ds(base + m, n, stride=CONV_PHASES), :]
            for m in range(CONV_PHASES + CONV_K - 1)]
    return [w[2:3] * rows[q + 2] + w[1:2] * rows[q + 1] + w[0:1] * rows[q]
            for q in range(CONV_PHASES)]


def _phase_store(dst_ref, c, seg, n_seg, q, val, part=0, n_parts=1):
    part_rows = dst_ref.shape[1] // n_seg // n_parts
    start = (seg * n_parts + part) * part_rows + q
    dst_ref[c, pl.ds(start, part_rows // CONV_PHASES, stride=CONV_PHASES), :] = val


def _segments_per_tile(n_batch, tiles_per_seg, n_tiles):
    segs = n_batch // n_tiles if tiles_per_seg == 1 else 1
    assert segs * n_tiles == n_batch * tiles_per_seg
    return segs


def _last_tile_state(per_tile_state, n_batch):
    _, k, c = per_tile_state.shape
    return per_tile_state.reshape(n_batch, -1, k, c)[:, -1]


def _mixer_kernel(x_ref, xprev_ref, state_ref, n1g_ref, win_ref, vng_ref, ws_ref, bias_ref,
                  cw_ref, ona_ref, onb_ref, wout_ref, *rest, tiles_per_seg, emit_v):
    if emit_v:
        out_ref, newstate_ref, v_ref, mix_ref, pslab_ref, cb_ref, carry_ref = rest
    else:
        out_ref, newstate_ref, mix_ref, pslab_ref, cb_ref, carry_ref = rest
        v_ref = None
    tm = x_ref.shape[0]
    d_a = vng_ref.shape[1]
    n_pieces = d_a // MIX_PIECE
    n_seg = state_ref.shape[0]
    slabs_per_piece = MIX_PIECE // LANES
    heads_per_piece = MIX_PIECE // HD_A
    s = pl.program_id(0)
    tile_a = jnp.minimum(s, pl.num_programs(0) - 2)
    first_tile = tile_a % tiles_per_seg == 0

    @pl.when(s == 0)
    def _():
        mix_ref[...] = jnp.zeros_like(mix_ref)

    mix_prev = mix_ref[...]
    n_out = wout_ref.shape[0]

    def out_proj(n):
        return _dot(mix_prev, wout_ref[n])

    x = x_ref[...]
    h = (x * _rms_scale(x) * n1g_ref[...]).astype(_BF16)
    g_rows = ws_ref.shape[1]

    def in_proj(j):
        return [_dot(h, win_ref[k * n_pieces + j]) for k in range(5)]

    def gate_inputs(j, pu, pv):
        cols = slice(j * MIX_PIECE, (j + 1) * MIX_PIECE)
        u = _gelu_tanh(pu)
        v = _gelu_tanh(pv)
        heads = [v[:, hh * HD_A:(hh + 1) * HD_A] for hh in range(heads_per_piece)]
        vn = jnp.concatenate([vh * _rms_scale(vh) for vh in heads], axis=-1) * vng_ref[:, cols]
        if emit_v:
            v_ref[:, cols] = vn
        return u, vn.astype(_BF16)

    def spatial(j, vb):
        bias = bias_ref[:, j * MIX_PIECE:(j + 1) * MIX_PIECE]
        z_chunks = []
        for c in range(tm // g_rows):
            zh = [_dot(ws_ref[j * heads_per_piece + hh],
                       vb[c * g_rows:(c + 1) * g_rows, hh * HD_A:(hh + 1) * HD_A])
                  for hh in range(heads_per_piece)]
            z_chunks.append(jnp.concatenate(zh, axis=-1) + bias)
        return jnp.concatenate(z_chunks, axis=0)

    def gated_conv(j, gb, gc, hin):
        slab = pslab_ref.at[j * slabs_per_piece:(j + 1) * slabs_per_piece]
        _slab_store(slab, gc * hin, n_seg)
        _slab_conv_prepare(slab, state_ref, carry_ref, first_tile, tiles_per_seg, newstate_ref,
                           lane0=j * MIX_PIECE)
        cw = cw_ref[:, j * MIX_PIECE:(j + 1) * MIX_PIECE]
        for c in range(slabs_per_piece):
            for seg in range(n_seg):
                phases = _slab_conv_phases(slab, c, seg, n_seg, cw[:, c * LANES:(c + 1) * LANES])
                for q in range(CONV_PHASES):
                    _phase_store(cb_ref, j * slabs_per_piece + c, seg, n_seg, q, phases[q])
        cb = jnp.concatenate([cb_ref[j * slabs_per_piece + c] for c in range(slabs_per_piece)],
                             axis=-1)
        return gb * cb

    deltas = [out_proj(n) for n in range(n_out // 2)]
    projs = [in_proj(0)]
    a_pieces, b_pieces = [], []
    for j in range(n_pieces):
        if j + 1 < n_pieces:
            projs.append(in_proj(j + 1))
        else:
            deltas += [out_proj(n) for n in range(n_out // 2, 3 * n_out // 4)]
        pu, pv, gb, gc, hin = projs[j]
        u, vb = gate_inputs(j, pu, pv)
        a_pieces.append(u * spatial(j, vb))
        b_pieces.append(gated_conv(j, gb, gc, hin))
    deltas += [out_proj(n) for n in range(3 * n_out // 4, n_out)]
    out_ref[...] = xprev_ref[...] + jnp.concatenate(deltas, axis=-1)

    a = jnp.concatenate(a_pieces, axis=-1)
    b = jnp.concatenate(b_pieces, axis=-1)
    mix = jnp.concatenate([a * _rms_scale(a) * ona_ref[...], b * _rms_scale(b) * onb_ref[...]],
                          axis=-1)
    mix_ref[...] = mix.astype(_BF16)


def _mixer_call(x, state, tiles_per_seg, emit_v, layer, n1_g, w_in_pieces, vnorm_g, ws_masked,
                bias_full, sconv_w, onorm_a_g, onorm_b_g, w_out_pieces):
    rows, d_model = x.shape
    n_batch, _, d_b = state.shape
    d_a = vnorm_g.shape[1]
    assert d_a == d_b and d_a % MIX_PIECE == 0
    assert w_in_pieces.shape[1:] == (5 * d_a // MIX_PIECE, d_model, MIX_PIECE)
    assert w_out_pieces.shape[1:] == (d_model // MIX_PIECE, 2 * d_a, MIX_PIECE)
    tm = min(MIX_ROW_TILE, rows)
    assert rows % tm == 0
    ni = rows // tm
    segs_per_tile = _segments_per_tile(n_batch, tiles_per_seg, ni)
    assert (tm // segs_per_tile) % (CONV_PHASES * SUBLANES) == 0
    g_rows = ws_masked.shape[1]
    assert (tm // segs_per_tile) % g_rows == 0

    def resident(shape):
        return pl.BlockSpec(shape, lambda s: (0,) * len(shape), pipeline_mode=pl.Buffered(1))

    def layer_resident(shape):
        return pl.BlockSpec((None,) + shape, lambda s: (layer,) + (0,) * len(shape),
                            pipeline_mode=pl.Buffered(1))

    def tile_a(s):
        return jnp.minimum(s, ni - 1)

    def tile_b(s):
        return jnp.maximum(s - 1, 0)

    state_shape = (segs_per_tile, CONV_K - 1, d_b)
    in_specs = [
        pl.BlockSpec((tm, d_model), lambda s: (tile_a(s), 0)),
        pl.BlockSpec((tm, d_model), lambda s: (tile_b(s), 0)),
        pl.BlockSpec(state_shape, lambda s: (tile_a(s) // tiles_per_seg, 0, 0)),
        resident((1, d_model)), layer_resident(w_in_pieces.shape[1:]), resident((1, d_a)),
        resident(ws_masked.shape), resident(bias_full.shape), resident(sconv_w.shape),
        resident((1, d_a)), resident((1, d_b)), layer_resident(w_out_pieces.shape[1:]),
    ]
    out_shape = [jax.ShapeDtypeStruct((rows, d_model), _F32),
                 jax.ShapeDtypeStruct((ni * segs_per_tile, CONV_K - 1, d_b), _F32)]
    out_specs = [pl.BlockSpec((tm, d_model), lambda s: (tile_b(s), 0)),
                 pl.BlockSpec(state_shape, lambda s: (tile_a(s), 0, 0))]
    if emit_v:
        out_shape.append(jax.ShapeDtypeStruct((rows, d_a), _F32))
        out_specs.append(pl.BlockSpec((tm, d_a), lambda s: (tile_a(s), 0)))
    scratch = [
        pltpu.VMEM((tm, 2 * d_a), _BF16),
        pltpu.VMEM((d_b // LANES, tm + segs_per_tile * HEAD_ROWS, LANES), _F32),
        pltpu.VMEM((d_b // LANES, tm, LANES), _F32),
        pltpu.VMEM((CONV_K - 1, d_b), _F32),
    ]
    outs = pl.pallas_call(
        functools.partial(_mixer_kernel, tiles_per_seg=tiles_per_seg, emit_v=emit_v),
        grid=(ni + 1,), in_specs=in_specs, out_specs=out_specs, out_shape=out_shape,
        scratch_shapes=scratch, name="mixer",
        compiler_params=pltpu.CompilerParams(
            dimension_semantics=("arbitrary",),
            vmem_limit_bytes=VMEM_LIMIT_BYTES),
    )(x, x, state, n1_g, w_in_pieces, vnorm_g, ws_masked, bias_full, sconv_w, onorm_a_g,
      onorm_b_g, w_out_pieces)
    return (outs[0], _last_tile_state(outs[1], n_batch)) + tuple(outs[2:])


def _ffn_kernel(x_hbm, stg_ref, stv_ref, n2g_ref, wg_ref, wv_ref, cwg_ref, cwv_ref, wd_ref,
                fg_ref, out_ref, newg_ref, newv_ref, h_ref, upa_ref, upb_ref, acta_ref, actb_ref,
                carry_ref, xbuf_ref, xsem, *, n_ff_tiles, tiles_per_seg, final_norm):
    s = pl.program_id(0)
    n_items = pl.num_programs(0) - 2
    tm = xbuf_ref.shape[0]
    n_tiles = x_hbm.shape[0] // tm
    ia = s // n_ff_tiles
    ja = s % n_ff_tiles
    item_b = jnp.clip(s - 1, 0, n_items - 1)
    ib = item_b // n_ff_tiles
    jb = item_b % n_ff_tiles
    jc = jnp.maximum(s - 2, 0) % n_ff_tiles

    def x_copy(tile):
        rows = pl.ds(pl.multiple_of(tile * tm, tm), tm)
        return pltpu.make_async_copy(x_hbm.at[rows], xbuf_ref, xsem.at[0])

    @pl.when(s == 0)
    def _():
        upb_ref[...] = jnp.zeros_like(upb_ref)
        actb_ref[...] = jnp.zeros_like(actb_ref)
        x_copy(0).start()

    @pl.when(jnp.logical_and(ja == 3, ia + 1 < n_tiles))
    def _():
        x_copy(ia + 1).start()

    @pl.when(jnp.logical_and(ja == 0, s < n_items))
    def _():
        x_copy(ia).wait()
        x = xbuf_ref[...]
        h_ref[...] = (x * _rms_scale(x) * n2g_ref[...]).astype(_BF16)

    @pl.when(jc == 0)
    def _():
        out_ref[...] = xbuf_ref[...]

    n_seg = stg_ref.shape[0]
    first_tile = ib % tiles_per_seg == 0
    seg_rows = tm // n_seg
    if n_seg == 1:
        units_of_part = [[(0, r, FFN_ROW_PARTS)] for r in range(FFN_ROW_PARTS)]
    else:
        per = n_seg // FFN_ROW_PARTS
        units_of_part = [[(seg, 0, 1) for seg in range(r * per, (r + 1) * per)]
                         for r in range(FFN_ROW_PARTS)]

    def stages(up_new, up_old, act_new, act_old):
        upg, upv = up_old.at[0], up_old.at[1]
        _slab_conv_prepare(upg, stg_ref, carry_ref.at[0, jb], first_tile, tiles_per_seg, newg_ref)
        _slab_conv_prepare(upv, stv_ref, carry_ref.at[1, jb], first_tile, tiles_per_seg, newv_ref)
        cwg = cwg_ref[...]
        cwv = cwv_ref[...]
        n_slabs = act_new.shape[0]

        def vector_work(c, r):
            lanes = slice(c * LANES, (c + 1) * LANES)
            for seg, part, n_parts in units_of_part[r]:
                g = _slab_conv_phases(upg, c, seg, n_seg, cwg[:, lanes], part, n_parts)
                val = _slab_conv_phases(upv, c, seg, n_seg, cwv[:, lanes], part, n_parts)
                for q in range(CONV_PHASES):
                    _phase_store(act_new, c, seg, n_seg, q, _silu(g[q]) * val[q], part, n_parts)

        def store_up(slab_ref, val, r):
            row = 0
            for seg, part, n_parts in units_of_part[r]:
                n = seg_rows // n_parts
                r0 = seg * (HEAD_ROWS + seg_rows) + HEAD_ROWS + part * n
                for c in range(slab_ref.shape[0]):
                    slab_ref[c, r0:r0 + n, :] = val[row:row + n, c * LANES:(c + 1) * LANES]
                row += n

        for r in range(FFN_ROW_PARTS):
            rows = slice(r * tm // FFN_ROW_PARTS, (r + 1) * tm // FFN_ROW_PARTS)
            act = jnp.concatenate([act_old[c, rows, :] for c in range(n_slabs)], axis=-1)
            out_ref[rows, :] += _dot(act.astype(_BF16), wd_ref[...])
            for c in range(n_slabs // 2):
                vector_work(c, r)
            h = h_ref[rows, :]
            store_up(up_new.at[0], _dot(h, wg_ref[...]), r)
            for c in range(n_slabs // 2, 3 * n_slabs // 4):
                vector_work(c, r)
            store_up(up_new.at[1], _dot(h, wv_ref[...]), r)
            for c in range(3 * n_slabs // 4, n_slabs):
                vector_work(c, r)

    @pl.when(s % 2 == 0)
    def _():
        stages(upa_ref, upb_ref, acta_ref, actb_ref)

    @pl.when(s % 2 == 1)
    def _():
        stages(upb_ref, upa_ref, actb_ref, acta_ref)

    if final_norm:
        @pl.when(jnp.logical_and(jc == n_ff_tiles - 1, s > 1))
        def _():
            y = out_ref[...]
            out_ref[...] = y * _rms_scale(y) * fg_ref[...]


def _ffn_call(x, state, tiles_per_seg, final_norm, layer, n2_g, ffn_up, ffn_conv_w, ffn_down,
              final_g):
    rows, d_model = x.shape
    n_batch = state.shape[0]
    d_ff = ffn_down.shape[1]
    assert ffn_up.shape[1:] == (d_model, 2 * d_ff)
    assert d_ff % FF_TILE == 0
    tm = min(FFN_ROW_TILE, rows)
    assert rows % tm == 0
    ni, nj = rows // tm, d_ff // FF_TILE
    segs_per_tile = _segments_per_tile(n_batch, tiles_per_seg, ni)
    assert nj >= 4
    n_items = ni * nj

    def item_a(s):
        t = jnp.minimum(s, n_items - 1)
        return t // nj, t % nj

    def item_b(s):
        t = jnp.clip(s - 1, 0, n_items - 1)
        return t // nj, t % nj

    def item_c(s):
        t = jnp.maximum(s - 2, 0)
        return t // nj, t % nj

    def spec_a(shape, fn, **kw):
        return pl.BlockSpec(shape, lambda s: fn(*item_a(s)), **kw)

    def spec_b(shape, fn):
        return pl.BlockSpec(shape, lambda s: fn(*item_b(s)))

    def spec_c(shape, fn):
        return pl.BlockSpec(shape, lambda s: fn(*item_c(s)))

    full_vec = pl.BlockSpec((1, d_model), lambda s: (0, 0))
    state_shape = (segs_per_tile, CONV_K - 1, FF_TILE)
    in_specs = [
        pl.BlockSpec(memory_space=pl.ANY),
        spec_b(state_shape, lambda i, j: (i // tiles_per_seg, 0, j)),
        spec_b(state_shape, lambda i, j: (i // tiles_per_seg, 0, nj + j)),
        full_vec,
        spec_a((None, d_model, FF_TILE), lambda i, j: (layer, 0, j)),
        spec_a((None, d_model, FF_TILE), lambda i, j: (layer, 0, nj + j)),
        spec_b((CONV_K, FF_TILE), lambda i, j: (0, j)),
        spec_b((CONV_K, FF_TILE), lambda i, j: (0, nj + j)),
        spec_c((None, FF_TILE, d_model), lambda i, j: (layer, j, 0)),
        full_vec,
    ]
    assert (tm // FFN_ROW_PARTS) % (CONV_PHASES * SUBLANES) == 0
    assert segs_per_tile == 1 or segs_per_tile % FFN_ROW_PARTS == 0
    slab_shape = (FF_TILE // LANES, tm + segs_per_tile * HEAD_ROWS, LANES)
    half_state = jax.ShapeDtypeStruct((ni * segs_per_tile, CONV_K - 1, d_ff), _F32)
    new_state_spec = spec_b(state_shape, lambda i, j: (i, 0, j))
    out_shape = [jax.ShapeDtypeStruct((rows, d_model), _F32), half_state, half_state]
    out_specs = [spec_c((tm, d_model), lambda i, j: (i, 0)), new_state_spec, new_state_spec]
    act_shape = (FF_TILE // LANES, tm, LANES)
    scratch = [pltpu.VMEM((tm, d_model), _BF16),
               pltpu.VMEM((2,) + slab_shape, _F32), pltpu.VMEM((2,) + slab_shape, _F32),
               pltpu.VMEM(act_shape, _F32), pltpu.VMEM(act_shape, _F32),
               pltpu.VMEM((2, nj, CONV_K - 1, FF_TILE), _F32),
               pltpu.VMEM((tm, d_model), _F32), pltpu.SemaphoreType.DMA((1,))]
    y, new_g, new_v = pl.pallas_call(
        functools.partial(_ffn_kernel, n_ff_tiles=nj, tiles_per_seg=tiles_per_seg,
                          final_norm=final_norm),
        grid=(n_items + 2,), in_specs=in_specs, out_specs=out_specs, out_shape=out_shape,
        scratch_shapes=scratch, name="ffn",
        compiler_params=pltpu.CompilerParams(
            dimension_semantics=("arbitrary",),
            vmem_limit_bytes=VMEM_LIMIT_BYTES),
    )(x, state, state, n2_g, ffn_up, ffn_up, ffn_conv_w, ffn_conv_w, ffn_down, final_g)
    return y, jnp.concatenate([_last_tile_state(new_g, n_batch),
                               _last_tile_state(new_v, n_batch)], axis=-1)


def _spatial_operands(ws, bs, t):
    blk = jnp.arange(GMLP_CHUNK) // CHUNK
    mask = blk[:, None] >= blk[None, :]
    wm = jnp.where(mask[None], ws, 0.0)[:, :t, :t].astype(_BF16)
    bias = jnp.repeat(jnp.transpose(bs)[:t], HD_A, axis=1)
    return wm, bias


def _column_pieces(w, width):
    depth, k, n = w.shape
    return jnp.transpose(w.astype(_BF16).reshape(depth, k, n // width, width), (0, 2, 1, 3))


def kernel(x_prompt, x_sample, state_sconv, state_ffnconv, n1_g, w_in, vnorm_g, gmlp_ws, gmlp_bs,
           sconv_w, onorm_a_g, onorm_b_g, w_out, n2_g, ffn_up, ffn_conv_w, ffn_down, final_g):
    depth = w_in.shape[0]
    batch, seq, d_model = x_prompt.shape
    dec_batch, dec_seq, _ = x_sample.shape
    d_b = state_sconv.shape[-1]
    d_ff2 = state_ffnconv.shape[-1]
    assert seq % FFN_ROW_TILE == 0 and seq % MIX_ROW_TILE == 0 and MIX_ROW_TILE % GMLP_CHUNK == 0
    assert dec_seq <= GMLP_CHUNK

    xp = x_prompt.reshape(batch * seq, d_model)
    xs = x_sample.reshape(dec_batch * dec_seq, d_model)
    zero_sconv = jnp.zeros((batch, CONV_K - 1, d_b), _F32)
    zero_ffn = jnp.zeros((batch, CONV_K - 1, d_ff2), _F32)
    fg = final_g.reshape(1, d_model)
    w_in_pieces = _column_pieces(w_in, MIX_PIECE)
    w_out_pieces = _column_pieces(w_out, MIX_PIECE)
    ffn_up_b = ffn_up.astype(_BF16)
    ffn_down_b = ffn_down.astype(_BF16)

    p_sconv, p_ffn, s_v, s_sconv, s_ffn = [], [], [], [], []
    for l in range(depth):
        last = l == depth - 1
        wm_p, bias_p = _spatial_operands(gmlp_ws[l], gmlp_bs[l], GMLP_CHUNK)
        wm_s, bias_s = _spatial_operands(gmlp_ws[l], gmlp_bs[l], dec_seq)
        mixer_w = (l, n1_g[l].reshape(1, -1), w_in_pieces, vnorm_g[l].reshape(1, -1))
        mixer_w2 = (sconv_w[l], onorm_a_g[l].reshape(1, -1), onorm_b_g[l].reshape(1, -1),
                    w_out_pieces)
        ffn_w = (l, n2_g[l].reshape(1, -1), ffn_up_b, ffn_conv_w[l], ffn_down_b, fg)

        xp, ps = _mixer_call(xp, zero_sconv, seq // MIX_ROW_TILE, False, *mixer_w, wm_p, bias_p,
                             *mixer_w2)
        xp, pf = _ffn_call(xp, zero_ffn, seq // FFN_ROW_TILE, last, *ffn_w)
        xs, ss, sv = _mixer_call(xs, state_sconv[l], 1, True, *mixer_w, wm_s, bias_s, *mixer_w2)
        xs, sf = _ffn_call(xs, state_ffnconv[l], 1, last, *ffn_w)
        p_sconv.append(ps)
        p_ffn.append(pf)
        s_v.append(sv.reshape(dec_batch, dec_seq, -1))
        s_sconv.append(ss)
        s_ffn.append(sf)

    return (xp.reshape(batch, seq, d_model), xs.reshape(dec_batch, dec_seq, d_model),
            jnp.stack(p_sconv), jnp.stack(p_ffn), jnp.stack(s_v), jnp.stack(s_sconv),
            jnp.stack(s_ffn))
```

```python
import functools
import math

import jax
import jax.numpy as jnp
from jax import lax
from jax.experimental import pallas as pl
from jax.experimental.pallas import tpu as pltpu

EPS = 1e-6
H_A = 8
HD_A = 128
CHUNK = 64
GMLP_CHUNK = 128
CONV_K = 3
SUBLANES = 8
LANES = 128
CONV_PHASES = 4
HEAD_ROWS = SUBLANES

MIX_ROW_TILE = 256
MIX_PIECE = 256
FFN_ROW_TILE = 1024
FF_TILE = 512
FFN_ROW_PARTS = 4
VMEM_LIMIT_BYTES = 58 * 1024 * 1024

_BF16 = jnp.bfloat16
_F32 = jnp.float32


def _dot(a, b):
    return jnp.dot(a, b, preferred_element_type=_F32)


def _gelu_tanh(x):
    c = math.sqrt(2.0 / math.pi)
    return 0.5 * x * (1.0 + jnp.tanh(c * (x + 0.044715 * (x * x * x))))


def _silu(x):
    hx = 0.5 * x
    return hx + hx * jnp.tanh(hx)


def _rms_scale(x):
    return lax.rsqrt(jnp.mean(x * x, axis=-1, keepdims=True) + EPS)


def _slab_seg_rows(slab_ref, n_seg):
    return slab_ref.shape[1] // n_seg - HEAD_ROWS


def _slab_store(slab_ref, val, n_seg):
    seg_rows = _slab_seg_rows(slab_ref, n_seg)
    for c in range(slab_ref.shape[0]):
        for s in range(n_seg):
            r0 = s * (HEAD_ROWS + seg_rows) + HEAD_ROWS
            slab_ref[c, r0:r0 + seg_rows, :] = val[s * seg_rows:(s + 1) * seg_rows,
                                                   c * LANES:(c + 1) * LANES]


def _slab_conv_prepare(slab_ref, state_ref, carry_ref, first_tile, tiles_per_seg, new_state_ref,
                       lane0=0):
    n_seg = state_ref.shape[0]
    seg_rows = _slab_seg_rows(slab_ref, n_seg)
    width = slab_ref.shape[0] * LANES
    for s in range(n_seg):
        if tiles_per_seg == 1:
            prev = state_ref[s, :, lane0:lane0 + width]
        else:
            prev = jnp.where(first_tile, state_ref[s, :, lane0:lane0 + width],
                             carry_ref[:, lane0:lane0 + width])
        r0 = s * (HEAD_ROWS + seg_rows) + HEAD_ROWS
        lasts = []
        for c in range(slab_ref.shape[0]):
            slab_ref[c, r0 - (CONV_K - 1):r0, :] = prev[:, c * LANES:(c + 1) * LANES]
            lasts.append(slab_ref[c, r0 + seg_rows - (CONV_K - 1):r0 + seg_rows, :])
        last = jnp.concatenate(lasts, axis=-1)
        new_state_ref[s, :, lane0:lane0 + width] = last
        if tiles_per_seg != 1:
            carry_ref[:, lane0:lane0 + width] = last


def _slab_conv_phases(slab_ref, c, seg, n_seg, w, part=0, n_parts=1):
    seg_rows = _slab_seg_rows(slab_ref, n_seg)
    part_rows = seg_rows // n_parts
    base = seg * (HEAD_ROWS + seg_rows) + HEAD_ROWS + part * part_rows - (CONV_K - 1)
    n = part_rows // CONV_PHASES
    rows = [slab_ref[c, pl.ds(base + m, n, stride=CONV_PHASES), :]
            for m in range(CONV_PHASES + CONV_K - 1)]
    return [w[2:3] * rows[q + 2] + w[1:2] * rows[q + 1] + w[0:1] * rows[q]
            for q in range(CONV_PHASES)]


def _phase_store(dst_ref, c, seg, n_seg, q, val, part=0, n_parts=1):
    part_rows = dst_ref.shape[1] // n_seg // n_parts
    start = (seg * n_parts + part) * part_rows + q
    dst_ref[c, pl.ds(start, part_rows // CONV_PHASES, stride=CONV_PHASES), :] = val


def _segments_per_tile(n_batch, tiles_per_seg, n_tiles):
    segs = n_batch // n_tiles if tiles_per_seg == 1 else 1
    assert segs * n_tiles == n_batch * tiles_per_seg
    return segs


def _last_tile_state(per_tile_state, n_batch):
    _, k, c = per_tile_state.shape
    return per_tile_state.reshape(n_batch, -1, k, c)[:, -1]


def _mixer_kernel(x_ref, xprev_ref, state_ref, n1g_ref, win_ref, vng_ref, ws_ref, bias_ref,
                  cw_ref, ona_ref, onb_ref, wout_ref, *rest, tiles_per_seg, emit_v):
    if emit_v:
        out_ref, newstate_ref, v_ref, mix_ref, pslab_ref, cb_ref, carry_ref = rest
    else:
        out_ref, newstate_ref, mix_ref, pslab_ref, cb_ref, carry_ref = rest
        v_ref = None
    tm = x_ref.shape[0]
    d_a = vng_ref.shape[1]
    n_pieces = d_a // MIX_PIECE
    n_seg = state_ref.shape[0]
    slabs_per_piece = MIX_PIECE // LANES
    heads_per_piece = MIX_PIECE // HD_A
    s = pl.program_id(0)
    tile_a = jnp.minimum(s, pl.num_programs(0) - 2)
    first_tile = tile_a % tiles_per_seg == 0

    @pl.when(s == 0)
    def _():
        mix_ref[...] = jnp.zeros_like(mix_ref)

    mix_prev = mix_ref[...]
    n_out = wout_ref.shape[0]

    def out_proj(n):
        return _dot(mix_prev, wout_ref[n])

    x = x_ref[...]
    h = (x * _rms_scale(x) * n1g_ref[...]).astype(_BF16)
    g_rows = ws_ref.shape[1]

    def in_proj(j):
        return [_dot(h, win_ref[k * n_pieces + j]) for k in range(5)]

    def gate_inputs(j, pu, pv):
        cols = slice(j * MIX_PIECE, (j + 1) * MIX_PIECE)
        u = _gelu_tanh(pu)
        v = _gelu_tanh(pv)
        heads = [v[:, hh * HD_A:(hh + 1) * HD_A] for hh in range(heads_per_piece)]
        vn = jnp.concatenate([vh * _rms_scale(vh) for vh in heads], axis=-1) * vng_ref[:, cols]
        if emit_v:
            v_ref[:, cols] = vn
        return u, vn.astype(_BF16)

    def spatial(j, vb):
        bias = bias_ref[:, j * MIX_PIECE:(j + 1) * MIX_PIECE]
        z_chunks = []
        for c in range(tm // g_rows):
            zh = [_dot(ws_ref[j * heads_per_piece + hh],
                       vb[c * g_rows:(c + 1) * g_rows, hh * HD_A:(hh + 1) * HD_A])
                  for hh in range(heads_per_piece)]
            z_chunks.append(jnp.concatenate(zh, axis=-1) + bias)
        return jnp.concatenate(z_chunks, axis=0)

    def gated_conv(j, gb, gc, hin):
        slab = pslab_ref.at[j * slabs_per_piece:(j + 1) * slabs_per_piece]
        _slab_store(slab, gc * hin, n_seg)
        _slab_conv_prepare(slab, state_ref, carry_ref, first_tile, tiles_per_seg, newstate_ref,
                           lane0=j * MIX_PIECE)
        cw = cw_ref[:, j * MIX_PIECE:(j + 1) * MIX_PIECE]
        for c in range(slabs_per_piece):
            for seg in range(n_seg):
                phases = _slab_conv_phases(slab, c, seg, n_seg, cw[:, c * LANES:(c + 1) * LANES])
                for q in range(CONV_PHASES):
                    _phase_store(cb_ref, j * slabs_per_piece + c, seg, n_seg, q, phases[q])
        cb = jnp.concatenate([cb_ref[j * slabs_per_piece + c] for c in range(slabs_per_piece)],
                             axis=-1)
        return gb * cb

    deltas = [out_proj(n) for n in range(n_out // 2)]
    projs = [in_proj(0)]
    a_pieces, b_pieces = [], []
    for j in range(n_pieces):
        if j + 1 < n_pieces:
            projs.append(in_proj(j + 1))
        else:
            deltas += [out_proj(n) for n in range(n_out // 2, 3 * n_out // 4)]
        pu, pv, gb, gc, hin = projs[j]
        u, vb = gate_inputs(j, pu, pv)
        a_pieces.append(u * spatial(j, vb))
        b_pieces.append(gated_conv(j, gb, gc, hin))
    deltas += [out_proj(n) for n in range(3 * n_out // 4, n_out)]
    out_ref[...] = xprev_ref[...] + jnp.concatenate(deltas, axis=-1)

    a = jnp.concatenate(a_pieces, axis=-1)
    b = jnp.concatenate(b_pieces, axis=-1)
    mix = jnp.concatenate([a * _rms_scale(a) * ona_ref[...], b * _rms_scale(b) * onb_ref[...]],
                          axis=-1)
    mix_ref[...] = mix.astype(_BF16)


def _mixer_call(x, state, tiles_per_seg, emit_v, layer, n1_g, w_in_pieces, vnorm_g, ws_masked,
                bias_full, sconv_w, onorm_a_g, onorm_b_g, w_out_pieces):
    rows, d_model = x.shape
    n_batch, _, d_b = state.shape
    d_a = vnorm_g.shape[1]
    assert d_a == d_b and d_a % MIX_PIECE == 0
    assert w_in_pieces.shape[1:] == (5 * d_a // MIX_PIECE, d_model, MIX_PIECE)
    assert w_out_pieces.shape[1:] == (d_model // MIX_PIECE, 2 * d_a, MIX_PIECE)
    tm = min(MIX_ROW_TILE, rows)
    assert rows % tm == 0
    ni = rows // tm
    segs_per_tile = _segments_per_tile(n_batch, tiles_per_seg, ni)
    assert (tm // segs_per_tile) % (CONV_PHASES * SUBLANES) == 0
    g_rows = ws_masked.shape[1]
    assert (tm // segs_per_tile) % g_rows == 0

    def resident(shape):
        return pl.BlockSpec(shape, lambda s: (0,) * len(shape), pipeline_mode=pl.Buffered(1))

    def layer_resident(shape):
        return pl.BlockSpec((None,) + shape, lambda s: (layer,) + (0,) * len(shape),
                            pipeline_mode=pl.Buffered(1))

    def tile_a(s):
        return jnp.minimum(s, ni - 1)

    def tile_b(s):
        return jnp.maximum(s - 1, 0)

    state_shape = (segs_per_tile, CONV_K - 1, d_b)
    in_specs = [
        pl.BlockSpec((tm, d_model), lambda s: (tile_a(s), 0)),
        pl.BlockSpec((tm, d_model), lambda s: (tile_b(s), 0)),
        pl.BlockSpec(state_shape, lambda s: (tile_a(s) // tiles_per_seg, 0, 0)),
        resident((1, d_model)), layer_resident(w_in_pieces.shape[1:]), resident((1, d_a)),
        resident(ws_masked.shape), resident(bias_full.shape), resident(sconv_w.shape),
        resident((1, d_a)), resident((1, d_b)), layer_resident(w_out_pieces.shape[1:]),
    ]
    out_shape = [jax.ShapeDtypeStruct((rows, d_model), _F32),
                 jax.ShapeDtypeStruct((ni * segs_per_tile, CONV_K - 1, d_b), _F32)]
    out_specs = [pl.BlockSpec((tm, d_model), lambda s: (tile_b(s), 0)),
                 pl.BlockSpec(state_shape, lambda s: (tile_a(s), 0, 0))]
    if emit_v:
        out_shape.append(jax.ShapeDtypeStruct((rows, d_a), _F32))
        out_specs.append(pl.BlockSpec((tm, d_a), lambda s: (tile_a(s), 0)))
    scratch = [
        pltpu.VMEM((tm, 2 * d_a), _BF16),
        pltpu.VMEM((d_b // LANES, tm + segs_per_tile * HEAD_ROWS, LANES), _F32),
        pltpu.VMEM((d_b // LANES, tm, LANES), _F32),
        pltpu.VMEM((CONV_K - 1, d_b), _F32),
    ]
    outs = pl.pallas_call(
        functools.partial(_mixer_kernel, tiles_per_seg=tiles_per_seg, emit_v=emit_v),
        grid=(ni + 1,), in_specs=in_specs, out_specs=out_specs, out_shape=out_shape,
        scratch_shapes=scratch, name="mixer",
        compiler_params=pltpu.CompilerParams(
            dimension_semantics=("arbitrary",),
            vmem_limit_bytes=VMEM_LIMIT_BYTES),
    )(x, x, state, n1_g, w_in_pieces, vnorm_g, ws_masked, bias_full, sconv_w, onorm_a_g,
      onorm_b_g, w_out_pieces)
    return (outs[0], _last_tile_state(outs[1], n_batch)) + tuple(outs[2:])


def _ffn_kernel(x_hbm, stg_ref, stv_ref, n2g_ref, wg_ref, wv_ref, cwg_ref, cwv_ref, wd_ref,
                fg_ref, out_ref, newg_ref, newv_ref, h_ref, upa_ref, upb_ref, acta_ref, actb_ref,
                carry_ref, xbuf_ref, xsem, *, n_ff_tiles, tiles_per_seg, final_norm):
    s = pl.program_id(0)
    n_items = pl.num_programs(0) - 2
    tm = xbuf_ref.shape[0]
    n_tiles = x_hbm.shape[0] // tm
    ia = s // n_ff_tiles
    ja = s % n_ff_tiles
    item_b = jnp.clip(s - 1, 0, n_items - 1)
    ib = item_b // n_ff_tiles
    jb = item_b % n_ff_tiles
    jc = jnp.maximum(s - 2, 0) % n_ff_tiles

    def x_copy(tile):
        rows = pl.ds(pl.multiple_of(tile * tm, tm), tm)
        return pltpu.make_async_copy(x_hbm.at[rows], xbuf_ref, xsem.at[0])

    @pl.when(s == 0)
    def _():
        upb_ref[...] = jnp.zeros_like(upb_ref)
        actb_ref[...] = jnp.zeros_like(actb_ref)
        x_copy(0).start()

    @pl.when(jnp.logical_and(ja == 3, ia + 1 < n_tiles))
    def _():
        x_copy(ia + 1).start()

    @pl.when(jnp.logical_and(ja == 0, s < n_items))
    def _():
        x_copy(ia).wait()
        x = xbuf_ref[...]
        h_ref[...] = (x * _rms_scale(x) * n2g_ref[...]).astype(_BF16)

    @pl.when(jc == 0)
    def _():
        out_ref[...] = xbuf_ref[...]

    n_seg = stg_ref.shape[0]
    first_tile = ib % tiles_per_seg == 0
    seg_rows = tm // n_seg
    if n_seg == 1:
        units_of_part = [[(0, r, FFN_ROW_PARTS)] for r in range(FFN_ROW_PARTS)]
    else:
        per = n_seg // FFN_ROW_PARTS
        units_of_part = [[(seg, 0, 1) for seg in range(r * per, (r + 1) * per)]
                         for r in range(FFN_ROW_PARTS)]

    def stages(up_new, up_old, act_new, act_old):
        upg, upv = up_old.at[0], up_old.at[1]
        _slab_conv_prepare(upg, stg_ref, carry_ref.at[0, jb], first_tile, tiles_per_seg, newg_ref)
        _slab_conv_prepare(upv, stv_ref, carry_ref.at[1, jb], first_tile, tiles_per_seg, newv_ref)
        cwg = cwg_ref[...]
        cwv = cwv_ref[...]
        n_slabs = act_new.shape[0]

        def vector_work(c, r):
            lanes = slice(c * LANES, (c + 1) * LANES)
            for seg, part, n_parts in units_of_part[r]:
                g = _slab_conv_phases(upg, c, seg, n_seg, cwg[:, lanes], part, n_parts)
                val = _slab_conv_phases(upv, c, seg, n_seg, cwv[:, lanes], part, n_parts)
                for q in range(CONV_PHASES):
                    _phase_store(act_new, c, seg, n_seg, q, _silu(g[q]) * val[q], part, n_parts)

        def store_up(slab_ref, val, r):
            row = 0
            for seg, part, n_parts in units_of_part[r]:
                n = seg_rows // n_parts
                r0 = seg * (HEAD_ROWS + seg_rows) + HEAD_ROWS + part * n
                for c in range(slab_ref.shape[0]):
                    slab_ref[c, r0:r0 + n, :] = val[row:row + n, c * LANES:(c + 1) * LANES]
                row += n

        for r in range(FFN_ROW_PARTS):
            rows = slice(r * tm // FFN_ROW_PARTS, (r + 1) * tm // FFN_ROW_PARTS)
            act = jnp.concatenate([act_old[c, rows, :] for c in range(n_slabs)], axis=-1)
            out_ref[rows, :] += _dot(act.astype(_BF16), wd_ref[...])
            for c in range(n_slabs // 2):
                vector_work(c, r)
            h = h_ref[rows, :]
            store_up(up_new.at[0], _dot(h, wg_ref[...]), r)
            for c in range(n_slabs // 2, 3 * n_slabs // 4):
                vector_work(c, r)
            store_up(up_new.at[1], _dot(h, wv_ref[...]), r)
            for c in range(3 * n_slabs // 4, n_slabs):
                vector_work(c, r)

    @pl.when(s % 2 == 0)
    def _():
        stages(upa_ref, upb_ref, acta_ref, actb_ref)

    @pl.when(s % 2 == 1)
    def _():
        stages(upb_ref, upa_ref, actb_ref, acta_ref)

    if final_norm:
        @pl.when(jnp.logical_and(jc == n_ff_tiles - 1, s > 1))
        def _():
            y = out_ref[...]
            out_ref[...] = y * _rms_scale(y) * fg_ref[...]


def _ffn_call(x, state, tiles_per_seg, final_norm, layer, n2_g, ffn_up, ffn_conv_w, ffn_down,
              final_g):
    rows, d_model = x.shape
    n_batch = state.shape[0]
    d_ff = ffn_down.shape[1]
    assert ffn_up.shape[1:] == (2 * d_ff // FF_TILE, d_model, FF_TILE)
    assert d_ff % FF_TILE == 0
    tm = min(FFN_ROW_TILE, rows)
    assert rows % tm == 0
    ni, nj = rows // tm, d_ff // FF_TILE
    segs_per_tile = _segments_per_tile(n_batch, tiles_per_seg, ni)
    assert nj >= 4
    n_items = ni * nj

    def item_a(s):
        t = jnp.minimum(s, n_items - 1)
        return t // nj, t % nj

    def item_b(s):
        t = jnp.clip(s - 1, 0, n_items - 1)
        return t // nj, t % nj

    def item_c(s):
        t = jnp.maximum(s - 2, 0)
        return t // nj, t % nj

    def spec_a(shape, fn, **kw):
        return pl.BlockSpec(shape, lambda s: fn(*item_a(s)), **kw)

    def spec_b(shape, fn):
        return pl.BlockSpec(shape, lambda s: fn(*item_b(s)))

    def spec_c(shape, fn):
        return pl.BlockSpec(shape, lambda s: fn(*item_c(s)))

    full_vec = pl.BlockSpec((1, d_model), lambda s: (0, 0))
    state_shape = (segs_per_tile, CONV_K - 1, FF_TILE)
    in_specs = [
        pl.BlockSpec(memory_space=pl.ANY),
        spec_b(state_shape, lambda i, j: (i // tiles_per_seg, 0, j)),
        spec_b(state_shape, lambda i, j: (i // tiles_per_seg, 0, nj + j)),
        full_vec,
        spec_a((None, None, d_model, FF_TILE), lambda i, j: (layer, j, 0, 0)),
        spec_a((None, None, d_model, FF_TILE), lambda i, j: (layer, nj + j, 0, 0)),
        spec_b((CONV_K, FF_TILE), lambda i, j: (0, j)),
        spec_b((CONV_K, FF_TILE), lambda i, j: (0, nj + j)),
        spec_c((None, FF_TILE, d_model), lambda i, j: (layer, j, 0)),
        full_vec,
    ]
    assert (tm // FFN_ROW_PARTS) % (CONV_PHASES * SUBLANES) == 0
    assert segs_per_tile == 1 or segs_per_tile % FFN_ROW_PARTS == 0
    slab_shape = (FF_TILE // LANES, tm + segs_per_tile * HEAD_ROWS, LANES)
    half_state = jax.ShapeDtypeStruct((ni * segs_per_tile, CONV_K - 1, d_ff), _F32)
    new_state_spec = spec_b(state_shape, lambda i, j: (i, 0, j))
    out_shape = [jax.ShapeDtypeStruct((rows, d_model), _F32), half_state, half_state]
    out_specs = [spec_c((tm, d_model), lambda i, j: (i, 0)), new_state_spec, new_state_spec]
    act_shape = (FF_TILE // LANES, tm, LANES)
    scratch = [pltpu.VMEM((tm, d_model), _BF16),
               pltpu.VMEM((2,) + slab_shape, _F32), pltpu.VMEM((2,) + slab_shape, _F32),
               pltpu.VMEM(act_shape, _F32), pltpu.VMEM(act_shape, _F32),
               pltpu.VMEM((2, nj, CONV_K - 1, FF_TILE), _F32),
               pltpu.VMEM((tm, d_model), _F32), pltpu.SemaphoreType.DMA((1,))]
    y, new_g, new_v = pl.pallas_call(
        functools.partial(_ffn_kernel, n_ff_tiles=nj, tiles_per_seg=tiles_per_seg,
                          final_norm=final_norm),
        grid=(n_items + 2,), in_specs=in_specs, out_specs=out_specs, out_shape=out_shape,
        scratch_shapes=scratch, name="ffn",
        compiler_params=pltpu.CompilerParams(
            dimension_semantics=("arbitrary",),
            vmem_limit_bytes=VMEM_LIMIT_BYTES),
    )(x, state, state, n2_g, ffn_up, ffn_up, ffn_conv_w, ffn_conv_w, ffn_down, final_g)
    return y, jnp.concatenate([_last_tile_state(new_g, n_batch),
                               _last_tile_state(new_v, n_batch)], axis=-1)


def _spatial_operands(ws, bs, t):
    blk = jnp.arange(GMLP_CHUNK) // CHUNK
    mask = blk[:, None] >= blk[None, :]
    wm = jnp.where(mask[None], ws, 0.0)[:, :t, :t].astype(_BF16)
    bias = jnp.repeat(jnp.transpose(bs)[:t], HD_A, axis=1)
    return wm, bias


def _column_pieces(w, width):
    depth, k, n = w.shape
    return jnp.transpose(w.astype(_BF16).reshape(depth, k, n // width, width), (0, 2, 1, 3))


def kernel(x_prompt, x_sample, state_sconv, state_ffnconv, n1_g, w_in, vnorm_g, gmlp_ws, gmlp_bs,
           sconv_w, onorm_a_g, onorm_b_g, w_out, n2_g, ffn_up, ffn_conv_w, ffn_down, final_g):
    depth = w_in.shape[0]
    batch, seq, d_model = x_prompt.shape
    dec_batch, dec_seq, _ = x_sample.shape
    d_b = state_sconv.shape[-1]
    d_ff2 = state_ffnconv.shape[-1]
    assert seq % FFN_ROW_TILE == 0 and seq % MIX_ROW_TILE == 0 and MIX_ROW_TILE % GMLP_CHUNK == 0
    assert dec_seq <= GMLP_CHUNK

    xp = x_prompt.reshape(batch * seq, d_model)
    xs = x_sample.reshape(dec_batch * dec_seq, d_model)
    zero_sconv = jnp.zeros((batch, CONV_K - 1, d_b), _F32)
    zero_ffn = jnp.zeros((batch, CONV_K - 1, d_ff2), _F32)
    fg = final_g.reshape(1, d_model)
    w_in_pieces = _column_pieces(w_in, MIX_PIECE)
    w_out_pieces = _column_pieces(w_out, MIX_PIECE)
    ffn_up_b = _column_pieces(ffn_up, FF_TILE)
    ffn_down_b = ffn_down.astype(_BF16)

    p_sconv, p_ffn, s_v, s_sconv, s_ffn = [], [], [], [], []
    for l in range(depth):
        last = l == depth - 1
        wm_p, bias_p = _spatial_operands(gmlp_ws[l], gmlp_bs[l], GMLP_CHUNK)
        wm_s, bias_s = _spatial_operands(gmlp_ws[l], gmlp_bs[l], dec_seq)
        mixer_w = (l, n1_g[l].reshape(1, -1), w_in_pieces, vnorm_g[l].reshape(1, -1))
        mixer_w2 = (sconv_w[l], onorm_a_g[l].reshape(1, -1), onorm_b_g[l].reshape(1, -1),
                    w_out_pieces)
        ffn_w = (l, n2_g[l].reshape(1, -1), ffn_up_b, ffn_conv_w[l], ffn_down_b, fg)

        xp, ps = _mixer_call(xp, zero_sconv, seq // MIX_ROW_TILE, False, *mixer_w, wm_p, bias_p,
                             *mixer_w2)
        xp, pf = _ffn_call(xp, zero_ffn, seq // FFN_ROW_TILE, last, *ffn_w)
        xs, ss, sv = _mixer_call(xs, state_sconv[l], 1, True, *mixer_w, wm_s, bias_s, *mixer_w2)
        xs, sf = _ffn_call(xs, state_ffnconv[l], 1, last, *ffn_w)
        p_sconv.append(ps)
        p_ffn.append(pf)
        s_v.append(sv.reshape(dec_batch, dec_seq, -1))
        s_sconv.append(ss)
        s_ffn.append(sf)

    return (xp.reshape(batch, seq, d_model), xs.reshape(dec_batch, dec_seq, d_model),
            jnp.stack(p_sconv), jnp.stack(p_ffn), jnp.stack(s_v), jnp.stack(s_sconv),
            jnp.stack(s_ffn))
```

```python
import functools
import math

import jax
import jax.numpy as jnp
from jax import lax
from jax.experimental import pallas as pl
from jax.experimental.pallas import tpu as pltpu

EPS = 1e-6
H_A = 8
HD_A = 128
CHUNK = 64
GMLP_CHUNK = 128
CONV_K = 3
SUBLANES = 8
LANES = 128
CONV_PHASES = 4
HEAD_ROWS = SUBLANES

MIX_ROW_TILE = 256
MIX_PIECE = 256
FFN_ROW_TILE = 1024
FF_TILE = 512
FFN_ROW_PARTS = 4
VMEM_LIMIT_BYTES = 60000 * 1024

_BF16 = jnp.bfloat16
_F32 = jnp.float32


def _dot(a, b):
    return jnp.dot(a, b, preferred_element_type=_F32)


def _gelu_tanh(x):
    c = math.sqrt(2.0 / math.pi)
    return 0.5 * x * (1.0 + jnp.tanh(c * (x + 0.044715 * (x * x * x))))


def _silu(x):
    hx = 0.5 * x
    return hx + hx * jnp.tanh(hx)


def _rms_scale(x):
    return lax.rsqrt(jnp.mean(x * x, axis=-1, keepdims=True) + EPS)


def _slab_seg_rows(slab_ref, n_seg):
    return slab_ref.shape[1] // n_seg - HEAD_ROWS


def _slab_store(slab_ref, val, n_seg):
    seg_rows = _slab_seg_rows(slab_ref, n_seg)
    for c in range(slab_ref.shape[0]):
        for s in range(n_seg):
            r0 = s * (HEAD_ROWS + seg_rows) + HEAD_ROWS
            slab_ref[c, r0:r0 + seg_rows, :] = val[s * seg_rows:(s + 1) * seg_rows,
                                                   c * LANES:(c + 1) * LANES]


def _slab_conv_prepare(slab_ref, state_ref, carry_ref, first_tile, tiles_per_seg, new_state_ref,
                       lane0=0):
    n_seg = state_ref.shape[0]
    seg_rows = _slab_seg_rows(slab_ref, n_seg)
    width = slab_ref.shape[0] * LANES
    for s in range(n_seg):
        if tiles_per_seg == 1:
            prev = state_ref[s, :, lane0:lane0 + width]
        else:
            prev = jnp.where(first_tile, state_ref[s, :, lane0:lane0 + width],
                             carry_ref[:, lane0:lane0 + width])
        r0 = s * (HEAD_ROWS + seg_rows) + HEAD_ROWS
        lasts = []
        for c in range(slab_ref.shape[0]):
            slab_ref[c, r0 - (CONV_K - 1):r0, :] = prev[:, c * LANES:(c + 1) * LANES]
            lasts.append(slab_ref[c, r0 + seg_rows - (CONV_K - 1):r0 + seg_rows, :])
        last = jnp.concatenate(lasts, axis=-1)
        new_state_ref[s, :, lane0:lane0 + width] = last
        if tiles_per_seg != 1:
            carry_ref[:, lane0:lane0 + width] = last


def _slab_conv_phases(slab_ref, c, seg, n_seg, w, part=0, n_parts=1):
    seg_rows = _slab_seg_rows(slab_ref, n_seg)
    part_rows = seg_rows // n_parts
    base = seg * (HEAD_ROWS + seg_rows) + HEAD_ROWS + part * part_rows - (CONV_K - 1)
    n = part_rows // CONV_PHASES
    rows = [slab_ref[c, pl.ds(base + m, n, stride=CONV_PHASES), :]
            for m in range(CONV_PHASES + CONV_K - 1)]
    return [w[2:3] * rows[q + 2] + w[1:2] * rows[q + 1] + w[0:1] * rows[q]
            for q in range(CONV_PHASES)]


def _phase_store(dst_ref, c, seg, n_seg, q, val, part=0, n_parts=1):
    part_rows = dst_ref.shape[1] // n_seg // n_parts
    start = (seg * n_parts + part) * part_rows + q
    dst_ref[c, pl.ds(start, part_rows // CONV_PHASES, stride=CONV_PHASES), :] = val


def _segments_per_tile(n_batch, tiles_per_seg, n_tiles):
    segs = n_batch // n_tiles if tiles_per_seg == 1 else 1
    assert segs * n_tiles == n_batch * tiles_per_seg
    return segs


def _last_tile_state(per_tile_state, n_batch):
    _, k, c = per_tile_state.shape
    return per_tile_state.reshape(n_batch, -1, k, c)[:, -1]


def _mixer_kernel(x_ref, xprev_ref, state_ref, n1g_ref, win_ref, vng_ref, ws_ref, bias_ref,
                  cw_ref, ona_ref, onb_ref, wout_ref, *rest, tiles_per_seg, emit_v):
    if emit_v:
        out_ref, newstate_ref, v_ref, mix_ref, pslab_ref, cb_ref, carry_ref = rest
    else:
        out_ref, newstate_ref, mix_ref, pslab_ref, cb_ref, carry_ref = rest
        v_ref = None
    tm = x_ref.shape[0]
    d_a = vng_ref.shape[1]
    n_pieces = d_a // MIX_PIECE
    n_seg = state_ref.shape[0]
    slabs_per_piece = MIX_PIECE // LANES
    heads_per_piece = MIX_PIECE // HD_A
    s = pl.program_id(0)
    tile_a = jnp.minimum(s, pl.num_programs(0) - 2)
    first_tile = tile_a % tiles_per_seg == 0

    @pl.when(s == 0)
    def _():
        mix_ref[...] = jnp.zeros_like(mix_ref)

    mix_prev = mix_ref[...]
    n_out = wout_ref.shape[0]

    def out_proj(n):
        return _dot(mix_prev, wout_ref[n])

    x = x_ref[...]
    h = (x * _rms_scale(x) * n1g_ref[...]).astype(_BF16)
    g_rows = ws_ref.shape[1]

    def in_proj(j):
        return [_dot(h, win_ref[k * n_pieces + j]) for k in range(5)]

    def gate_inputs(j, pu, pv):
        cols = slice(j * MIX_PIECE, (j + 1) * MIX_PIECE)
        u = _gelu_tanh(pu)
        v = _gelu_tanh(pv)
        heads = [v[:, hh * HD_A:(hh + 1) * HD_A] for hh in range(heads_per_piece)]
        vn = jnp.concatenate([vh * _rms_scale(vh) for vh in heads], axis=-1) * vng_ref[:, cols]
        if emit_v:
            v_ref[:, cols] = vn
        return u, vn.astype(_BF16)

    def spatial(j, vb):
        bias = bias_ref[:, j * MIX_PIECE:(j + 1) * MIX_PIECE]
        z_chunks = []
        for c in range(tm // g_rows):
            zh = [_dot(ws_ref[j * heads_per_piece + hh],
                       vb[c * g_rows:(c + 1) * g_rows, hh * HD_A:(hh + 1) * HD_A])
                  for hh in range(heads_per_piece)]
            z_chunks.append(jnp.concatenate(zh, axis=-1) + bias)
        return jnp.concatenate(z_chunks, axis=0)

    def gated_conv(j, gb, gc, hin):
        slab = pslab_ref.at[j * slabs_per_piece:(j + 1) * slabs_per_piece]
        _slab_store(slab, gc * hin, n_seg)
        _slab_conv_prepare(slab, state_ref, carry_ref, first_tile, tiles_per_seg, newstate_ref,
                           lane0=j * MIX_PIECE)
        cw = cw_ref[:, j * MIX_PIECE:(j + 1) * MIX_PIECE]
        for c in range(slabs_per_piece):
            for seg in range(n_seg):
                phases = _slab_conv_phases(slab, c, seg, n_seg, cw[:, c * LANES:(c + 1) * LANES])
                for q in range(CONV_PHASES):
                    _phase_store(cb_ref, j * slabs_per_piece + c, seg, n_seg, q, phases[q])
        cb = jnp.concatenate([cb_ref[j * slabs_per_piece + c] for c in range(slabs_per_piece)],
                             axis=-1)
        return gb * cb

    deltas = [out_proj(n) for n in range(n_out // 2)]
    projs = [in_proj(0)]
    a_pieces, b_pieces = [], []
    for j in range(n_pieces):
        if j + 1 < n_pieces:
            projs.append(in_proj(j + 1))
        else:
            deltas += [out_proj(n) for n in range(n_out // 2, 3 * n_out // 4)]
        pu, pv, gb, gc, hin = projs[j]
        u, vb = gate_inputs(j, pu, pv)
        a_pieces.append(u * spatial(j, vb))
        b_pieces.append(gated_conv(j, gb, gc, hin))
    deltas += [out_proj(n) for n in range(3 * n_out // 4, n_out)]
    out_ref[...] = xprev_ref[...] + jnp.concatenate(deltas, axis=-1)

    a = jnp.concatenate(a_pieces, axis=-1)
    b = jnp.concatenate(b_pieces, axis=-1)
    mix = jnp.concatenate([a * _rms_scale(a) * ona_ref[...], b * _rms_scale(b) * onb_ref[...]],
                          axis=-1)
    mix_ref[...] = mix.astype(_BF16)


def _mixer_call(x, state, tiles_per_seg, emit_v, layer, n1_g, w_in_pieces, vnorm_g, ws_masked,
                bias_full, sconv_w, onorm_a_g, onorm_b_g, w_out_pieces):
    rows, d_model = x.shape
    n_batch, _, d_b = state.shape
    d_a = vnorm_g.shape[1]
    assert d_a == d_b and d_a % MIX_PIECE == 0
    assert w_in_pieces.shape[1:] == (5 * d_a // MIX_PIECE, d_model, MIX_PIECE)
    assert w_out_pieces.shape[1:] == (d_model // MIX_PIECE, 2 * d_a, MIX_PIECE)
    tm = min(MIX_ROW_TILE, rows)
    assert rows % tm == 0
    ni = rows // tm
    segs_per_tile = _segments_per_tile(n_batch, tiles_per_seg, ni)
    assert (tm // segs_per_tile) % (CONV_PHASES * SUBLANES) == 0
    g_rows = ws_masked.shape[1]
    assert (tm // segs_per_tile) % g_rows == 0

    def resident(shape):
        return pl.BlockSpec(shape, lambda s: (0,) * len(shape), pipeline_mode=pl.Buffered(1))

    def layer_resident(shape):
        return pl.BlockSpec((None,) + shape, lambda s: (layer,) + (0,) * len(shape),
                            pipeline_mode=pl.Buffered(1))

    def tile_a(s):
        return jnp.minimum(s, ni - 1)

    def tile_b(s):
        return jnp.maximum(s - 1, 0)

    state_shape = (segs_per_tile, CONV_K - 1, d_b)
    in_specs = [
        pl.BlockSpec((tm, d_model), lambda s: (tile_a(s), 0)),
        pl.BlockSpec((tm, d_model), lambda s: (tile_b(s), 0)),
        pl.BlockSpec(state_shape, lambda s: (tile_a(s) // tiles_per_seg, 0, 0)),
        resident((1, d_model)), layer_resident(w_in_pieces.shape[1:]), resident((1, d_a)),
        resident(ws_masked.shape), resident(bias_full.shape), resident(sconv_w.shape),
        resident((1, d_a)), resident((1, d_b)), layer_resident(w_out_pieces.shape[1:]),
    ]
    out_shape = [jax.ShapeDtypeStruct((rows, d_model), _F32),
                 jax.ShapeDtypeStruct((ni * segs_per_tile, CONV_K - 1, d_b), _F32)]
    out_specs = [pl.BlockSpec((tm, d_model), lambda s: (tile_b(s), 0)),
                 pl.BlockSpec(state_shape, lambda s: (tile_a(s), 0, 0))]
    if emit_v:
        out_shape.append(jax.ShapeDtypeStruct((rows, d_a), _F32))
        out_specs.append(pl.BlockSpec((tm, d_a), lambda s: (tile_a(s), 0)))
    scratch = [
        pltpu.VMEM((tm, 2 * d_a), _BF16),
        pltpu.VMEM((d_b // LANES, tm + segs_per_tile * HEAD_ROWS, LANES), _F32),
        pltpu.VMEM((d_b // LANES, tm, LANES), _F32),
        pltpu.VMEM((CONV_K - 1, d_b), _F32),
    ]
    outs = pl.pallas_call(
        functools.partial(_mixer_kernel, tiles_per_seg=tiles_per_seg, emit_v=emit_v),
        grid=(ni + 1,), in_specs=in_specs, out_specs=out_specs, out_shape=out_shape,
        scratch_shapes=scratch, name="mixer",
        compiler_params=pltpu.CompilerParams(
            dimension_semantics=("arbitrary",),
            vmem_limit_bytes=VMEM_LIMIT_BYTES),
    )(x, x, state, n1_g, w_in_pieces, vnorm_g, ws_masked, bias_full, sconv_w, onorm_a_g,
      onorm_b_g, w_out_pieces)
    return (outs[0], _last_tile_state(outs[1], n_batch)) + tuple(outs[2:])


def _ffn_kernel(x_hbm, st_ref, n2g_ref, wg_ref, wv_ref, cw_ref, wd_ref,
                fg_ref, out_ref, newg_ref, newv_ref, h_ref, upa_ref, upb_ref, acta_ref, actb_ref,
                carry_ref, xbuf_ref, xsem, *, n_ff_tiles, n_seg, tiles_per_seg, final_norm):
    s = pl.program_id(0)
    n_items = pl.num_programs(0) - 2
    tm = xbuf_ref.shape[0]
    n_tiles = x_hbm.shape[0] // tm
    ia = s // n_ff_tiles
    ja = s % n_ff_tiles
    item_b = jnp.clip(s - 1, 0, n_items - 1)
    ib = item_b // n_ff_tiles
    jb = item_b % n_ff_tiles
    jc = jnp.maximum(s - 2, 0) % n_ff_tiles

    def x_copy(tile):
        rows = pl.ds(pl.multiple_of(tile * tm, tm), tm)
        return pltpu.make_async_copy(x_hbm.at[rows], xbuf_ref, xsem.at[0])

    @pl.when(s == 0)
    def _():
        upb_ref[...] = jnp.zeros_like(upb_ref)
        actb_ref[...] = jnp.zeros_like(actb_ref)
        x_copy(0).start()

    @pl.when(jnp.logical_and(ja == 3, ia + 1 < n_tiles))
    def _():
        x_copy(ia + 1).start()

    @pl.when(jnp.logical_and(ja == 0, s < n_items))
    def _():
        x_copy(ia).wait()
        x = xbuf_ref[...]
        h_ref[...] = (x * _rms_scale(x) * n2g_ref[...]).astype(_BF16)

    @pl.when(jc == 0)
    def _():
        out_ref[...] = xbuf_ref[...]

    first_tile = ib % tiles_per_seg == 0
    segs = pl.ds((ib // tiles_per_seg) * n_seg, n_seg)
    stg_ref, stv_ref = st_ref.at[0, jb, segs], st_ref.at[1, jb, segs]
    seg_rows = tm // n_seg
    if n_seg == 1:
        units_of_part = [[(0, r, FFN_ROW_PARTS)] for r in range(FFN_ROW_PARTS)]
    else:
        per = n_seg // FFN_ROW_PARTS
        units_of_part = [[(seg, 0, 1) for seg in range(r * per, (r + 1) * per)]
                         for r in range(FFN_ROW_PARTS)]

    def stages(up_new, up_old, act_new, act_old):
        upg, upv = up_old.at[0], up_old.at[1]
        _slab_conv_prepare(upg, stg_ref, carry_ref.at[0, jb], first_tile, tiles_per_seg, newg_ref)
        _slab_conv_prepare(upv, stv_ref, carry_ref.at[1, jb], first_tile, tiles_per_seg, newv_ref)
        cwg = cw_ref[0, jb]
        cwv = cw_ref[1, jb]
        n_slabs = act_new.shape[0]

        def vector_work(c, r):
            lanes = slice(c * LANES, (c + 1) * LANES)
            for seg, part, n_parts in units_of_part[r]:
                g = _slab_conv_phases(upg, c, seg, n_seg, cwg[:, lanes], part, n_parts)
                val = _slab_conv_phases(upv, c, seg, n_seg, cwv[:, lanes], part, n_parts)
                for q in range(CONV_PHASES):
                    _phase_store(act_new, c, seg, n_seg, q, _silu(g[q]) * val[q], part, n_parts)

        def store_up(slab_ref, val, r):
            row = 0
            for seg, part, n_parts in units_of_part[r]:
                n = seg_rows // n_parts
                r0 = seg * (HEAD_ROWS + seg_rows) + HEAD_ROWS + part * n
                for c in range(slab_ref.shape[0]):
                    slab_ref[c, r0:r0 + n, :] = val[row:row + n, c * LANES:(c + 1) * LANES]
                row += n

        for r in range(FFN_ROW_PARTS):
            rows = slice(r * tm // FFN_ROW_PARTS, (r + 1) * tm // FFN_ROW_PARTS)
            act = jnp.concatenate([act_old[c, rows, :] for c in range(n_slabs)], axis=-1)
            out_ref[rows, :] += _dot(act.astype(_BF16), wd_ref[...])
            for c in range(n_slabs // 2):
                vector_work(c, r)
            h = h_ref[rows, :]
            store_up(up_new.at[0], _dot(h, wg_ref[...]), r)
            for c in range(n_slabs // 2, 3 * n_slabs // 4):
                vector_work(c, r)
            store_up(up_new.at[1], _dot(h, wv_ref[...]), r)
            for c in range(3 * n_slabs // 4, n_slabs):
                vector_work(c, r)

    @pl.when(s % 2 == 0)
    def _():
        stages(upa_ref, upb_ref, acta_ref, actb_ref)

    @pl.when(s % 2 == 1)
    def _():
        stages(upb_ref, upa_ref, actb_ref, acta_ref)

    if final_norm:
        @pl.when(jnp.logical_and(jc == n_ff_tiles - 1, s > 1))
        def _():
            y = out_ref[...]
            out_ref[...] = y * _rms_scale(y) * fg_ref[...]


def _ffn_call(x, state, tiles_per_seg, final_norm, layer, n2_g, ffn_up, ffn_conv_w, ffn_down,
              final_g):
    rows, d_model = x.shape
    n_batch = state.shape[0]
    d_ff = ffn_down.shape[1]
    assert ffn_up.shape[1:] == (d_model, 2 * d_ff)
    assert d_ff % FF_TILE == 0
    tm = min(FFN_ROW_TILE, rows)
    assert rows % tm == 0
    ni, nj = rows // tm, d_ff // FF_TILE
    segs_per_tile = _segments_per_tile(n_batch, tiles_per_seg, ni)
    assert nj >= 4
    n_items = ni * nj

    def item_a(s):
        t = jnp.minimum(s, n_items - 1)
        return t // nj, t % nj

    def item_b(s):
        t = jnp.clip(s - 1, 0, n_items - 1)
        return t // nj, t % nj

    def item_c(s):
        t = jnp.maximum(s - 2, 0)
        return t // nj, t % nj

    def spec_a(shape, fn, **kw):
        return pl.BlockSpec(shape, lambda s: fn(*item_a(s)), **kw)

    def spec_b(shape, fn):
        return pl.BlockSpec(shape, lambda s: fn(*item_b(s)))

    def spec_c(shape, fn):
        return pl.BlockSpec(shape, lambda s: fn(*item_c(s)))

    def resident(shape):
        return pl.BlockSpec(shape, lambda s: (0,) * len(shape), pipeline_mode=pl.Buffered(1))

    st_tiles = jnp.transpose(state.reshape(n_batch, CONV_K - 1, 2, nj, FF_TILE), (2, 3, 0, 1, 4))
    cw_tiles = jnp.transpose(ffn_conv_w.reshape(CONV_K, 2, nj, FF_TILE), (1, 2, 0, 3))
    full_vec = pl.BlockSpec((1, d_model), lambda s: (0, 0))
    state_shape = (segs_per_tile, CONV_K - 1, FF_TILE)
    in_specs = [
        pl.BlockSpec(memory_space=pl.ANY),
        resident(st_tiles.shape),
        full_vec,
        spec_a((None, d_model, FF_TILE), lambda i, j: (layer, 0, j)),
        spec_a((None, d_model, FF_TILE), lambda i, j: (layer, 0, nj + j)),
        resident(cw_tiles.shape),
        spec_c((None, FF_TILE, d_model), lambda i, j: (layer, j, 0)),
        full_vec,
    ]
    assert (tm // FFN_ROW_PARTS) % (CONV_PHASES * SUBLANES) == 0
    assert segs_per_tile == 1 or segs_per_tile % FFN_ROW_PARTS == 0
    slab_shape = (FF_TILE // LANES, tm + segs_per_tile * HEAD_ROWS, LANES)
    half_state = jax.ShapeDtypeStruct((ni * segs_per_tile, CONV_K - 1, d_ff), _F32)
    new_state_spec = spec_b(state_shape, lambda i, j: (i, 0, j))
    out_shape = [jax.ShapeDtypeStruct((rows, d_model), _F32), half_state, half_state]
    out_specs = [spec_c((tm, d_model), lambda i, j: (i, 0)), new_state_spec, new_state_spec]
    act_shape = (FF_TILE // LANES, tm, LANES)
    scratch = [pltpu.VMEM((tm, d_model), _BF16),
               pltpu.VMEM((2,) + slab_shape, _F32), pltpu.VMEM((2,) + slab_shape, _F32),
               pltpu.VMEM(act_shape, _F32), pltpu.VMEM(act_shape, _F32),
               pltpu.VMEM((2, nj, CONV_K - 1, FF_TILE), _F32),
               pltpu.VMEM((tm, d_model), _F32), pltpu.SemaphoreType.DMA((1,))]
    y, new_g, new_v = pl.pallas_call(
        functools.partial(_ffn_kernel, n_ff_tiles=nj, n_seg=segs_per_tile,
                          tiles_per_seg=tiles_per_seg, final_norm=final_norm),
        grid=(n_items + 2,), in_specs=in_specs, out_specs=out_specs, out_shape=out_shape,
        scratch_shapes=scratch, name="ffn",
        compiler_params=pltpu.CompilerParams(
            dimension_semantics=("arbitrary",),
            vmem_limit_bytes=VMEM_LIMIT_BYTES),
    )(x, st_tiles, n2_g, ffn_up, ffn_up, cw_tiles, ffn_down, final_g)
    return y, jnp.concatenate([_last_tile_state(new_g, n_batch),
                               _last_tile_state(new_v, n_batch)], axis=-1)


def _spatial_operands(ws, bs, t):
    blk = jnp.arange(GMLP_CHUNK) // CHUNK
    mask = blk[:, None] >= blk[None, :]
    wm = jnp.where(mask[None], ws, 0.0)[:, :t, :t].astype(_BF16)
    bias = jnp.repeat(jnp.transpose(bs)[:t], HD_A, axis=1)
    return wm, bias


def _column_pieces(w, width):
    depth, k, n = w.shape
    return jnp.transpose(w.astype(_BF16).reshape(depth, k, n // width, width), (0, 2, 1, 3))


def kernel(x_prompt, x_sample, state_sconv, state_ffnconv, n1_g, w_in, vnorm_g, gmlp_ws, gmlp_bs,
           sconv_w, onorm_a_g, onorm_b_g, w_out, n2_g, ffn_up, ffn_conv_w, ffn_down, final_g):
    depth = w_in.shape[0]
    batch, seq, d_model = x_prompt.shape
    dec_batch, dec_seq, _ = x_sample.shape
    d_b = state_sconv.shape[-1]
    d_ff2 = state_ffnconv.shape[-1]
    assert seq % FFN_ROW_TILE == 0 and seq % MIX_ROW_TILE == 0 and MIX_ROW_TILE % GMLP_CHUNK == 0
    assert dec_seq <= GMLP_CHUNK

    xp = x_prompt.reshape(batch * seq, d_model)
    xs = x_sample.reshape(dec_batch * dec_seq, d_model)
    zero_sconv = jnp.zeros((batch, CONV_K - 1, d_b), _F32)
    zero_ffn = jnp.zeros((batch, CONV_K - 1, d_ff2), _F32)
    fg = final_g.reshape(1, d_model)
    w_in_pieces = _column_pieces(w_in, MIX_PIECE)
    w_out_pieces = _column_pieces(w_out, MIX_PIECE)
    ffn_up_b = ffn_up.astype(_BF16)
    ffn_down_b = ffn_down.astype(_BF16)

    p_sconv, p_ffn, s_v, s_sconv, s_ffn = [], [], [], [], []
    for l in range(depth):
        last = l == depth - 1
        wm_p, bias_p = _spatial_operands(gmlp_ws[l], gmlp_bs[l], GMLP_CHUNK)
        wm_s, bias_s = _spatial_operands(gmlp_ws[l], gmlp_bs[l], dec_seq)
        mixer_w = (l, n1_g[l].reshape(1, -1), w_in_pieces, vnorm_g[l].reshape(1, -1))
        mixer_w2 = (sconv_w[l], onorm_a_g[l].reshape(1, -1), onorm_b_g[l].reshape(1, -1),
                    w_out_pieces)
        ffn_w = (l, n2_g[l].reshape(1, -1), ffn_up_b, ffn_conv_w[l], ffn_down_b, fg)

        xp, ps = _mixer_call(xp, zero_sconv, seq // MIX_ROW_TILE, False, *mixer_w, wm_p, bias_p,
                             *mixer_w2)
        xp, pf = _ffn_call(xp, zero_ffn, seq // FFN_ROW_TILE, last, *ffn_w)
        xs, ss, sv = _mixer_call(xs, state_sconv[l], 1, True, *mixer_w, wm_s, bias_s, *mixer_w2)
        xs, sf = _ffn_call(xs, state_ffnconv[l], 1, last, *ffn_w)
        p_sconv.append(ps)
        p_ffn.append(pf)
        s_v.append(sv.reshape(dec_batch, dec_seq, -1))
        s_sconv.append(ss)
        s_ffn.append(sf)

    return (xp.reshape(batch, seq, d_model), xs.reshape(dec_batch, dec_seq, d_model),
            jnp.stack(p_sconv), jnp.stack(p_ffn), jnp.stack(s_v), jnp.stack(s_sconv),
            jnp.stack(s_ffn))
```
